```python
import jax, jax.numpy as jnp
from jax import lax
import numpy as np

D_MODEL = 1024
BATCH = 8
SEQ = 2048
DEPTH = 2
DEC_BATCH = 32
DEC_SEQ = 1
PAST_LEN = 16384
PAGE_SIZE = 128

MIX_WIDTH = D_MODEL
M_WIDTH = MIX_WIDTH // 2
M_HEADS = 4
M_HEAD_DIM = M_WIDTH // M_HEADS
SB_WIDTH = MIX_WIDTH - M_WIDTH
SB_HEADS = 8
SB_HEAD_DIM = SB_WIDTH // SB_HEADS
CONV_W = 4
MLSTM_CHUNK = 128
SB_BLOCK = 128
FFN_HIDDEN = -(-8 * D_MODEL // (3 * 256)) * 256
IN_COLS = 4 * M_WIDTH + 2 * M_HEADS + 3 * SB_WIDTH
RMS_EPS = 1e-6
SB_BIAS_INIT = -6.0

kernel_name = "hymba_mlstm_stickbreak_adaln_step"


def rmsnorm(x, g):
    xf = x.astype(jnp.float32)
    y = xf * lax.rsqrt(jnp.mean(xf * xf, axis=-1, keepdims=True) + RMS_EPS)
    return (y * g.astype(jnp.float32)).astype(x.dtype)


def causal_conv(u, buf, w, b):
    T = u.shape[1]
    full = jnp.concatenate([buf.astype(u.dtype), u], axis=1)
    out = b
    for j in range(CONV_W):
        out = out + full[:, j:j + T] * w[j]
    return out, full[:, -(CONV_W - 1):]


def mlstm_chunk(carry, xs):
    C, n, m = carry
    q, k, v, li, lf = xs
    L = q.shape[2]
    b = jnp.cumsum(lf, axis=-1)
    causal = jnp.tril(jnp.ones((L, L), dtype=bool))
    dmat = jnp.where(causal, b[..., :, None] - b[..., None, :] + li[..., None, :], -jnp.inf)
    m_inter = b + m[..., None]
    m_t = jnp.maximum(m_inter, jnp.max(dmat, axis=-1))
    w_inter = jnp.exp(m_inter - m_t)
    w_intra = jnp.exp(dmat - m_t[..., None]) * jnp.einsum('bhtd,bhsd->bhts', q, k)
    num = w_inter[..., None] * jnp.einsum('bhtd,bhde->bhte', q, C) + jnp.einsum('bhts,bhse->bhte', w_intra, v)
    den = w_inter * jnp.einsum('bhtd,bhd->bht', q, n) + jnp.sum(w_intra, axis=-1)
    h = num / jnp.maximum(jnp.abs(den), jnp.exp(-m_t))[..., None]
    b_last = b[..., -1]
    m_new = jnp.maximum(b_last + m, jnp.max(b_last[..., None] - b + li, axis=-1))
    g = jnp.exp(b_last[..., None] - b + li - m_new[..., None])
    decay = jnp.exp(b_last + m - m_new)
    C_new = decay[..., None, None] * C + jnp.einsum('bhs,bhsd,bhse->bhde', g, k, v)
    n_new = decay[..., None] * n + jnp.einsum('bhs,bhsd->bhd', g, k)
    return (C_new, n_new, m_new), h


def mlstm_seq(q, k, v, li, lf, state):
    B, T, H, _ = q.shape
    L = T if T <= MLSTM_CHUNK else MLSTM_CHUNK
    nc = T // L
    def split_tokens(a):
        a = jnp.swapaxes(a.astype(jnp.float32), 1, 2)
        a = a.reshape((B, H, nc, L) + a.shape[3:])
        return jnp.moveaxis(a, 2, 0)
    xs = (split_tokens(q), split_tokens(k), split_tokens(v), split_tokens(li), split_tokens(lf))
    state, h = lax.scan(mlstm_chunk, state, xs)
    h = jnp.moveaxis(h, 0, 2).reshape(B, H, T, -1)
    return jnp.swapaxes(h, 1, 2), state


def sb_block(q, q_pos, k, v, k_pos, bias):
    z = jnp.einsum('bqhd,bkhd->bhqk', q.astype(jnp.float32), k.astype(jnp.float32)) * (SB_HEAD_DIM ** -0.5)
    z = z + bias.astype(jnp.float32)[None, :, None, None]
    mask = k_pos[None, :] < q_pos[:, None]
    log_stay = jnp.where(mask, jax.nn.log_sigmoid(-z), 0.0)
    later = lax.cumsum(log_stay, axis=3, reverse=True) - log_stay
    a = jnp.where(mask, jnp.exp(jax.nn.log_sigmoid(z) + later), 0.0)
    return jnp.einsum('bhqk,bkhd->bqhd', a, v.astype(jnp.float32))


def sb_prompt(q, k, v, bias):
    B, T, H, D = q.shape
    nb = T // SB_BLOCK
    qb = jnp.swapaxes(q.reshape(B, nb, SB_BLOCK, H, D), 0, 1)
    pos = jnp.arange(T).reshape(nb, SB_BLOCK)
    k_pos = jnp.arange(T)
    out = lax.map(lambda a: sb_block(a[0], a[1], k, v, k_pos, bias), (qb, pos))
    return jnp.swapaxes(out, 0, 1).reshape(B, T, H, -1)


def make_sb_sample(k_past, v_past, past_len):
    def mix(q, k, v, bias):
        T = q.shape[1]
        keys = jnp.concatenate([k_past.astype(k.dtype), k], axis=1)
        vals = jnp.concatenate([v_past.astype(v.dtype), v], axis=1)
        q_pos = past_len + jnp.arange(T)
        k_pos = jnp.arange(past_len + T)
        return sb_block(q, q_pos, keys, vals, k_pos, bias)
    return mix


def trunk_layer(x, c, conv_buf, m_state, sb_mix, w_ada, b_ada, g1, w_in, b_gate, sb_bias, conv_w, conv_b,
                g_head, w_out, g2, w_gate, w_up, w_down):
    B, T, _ = x.shape
    mod = jax.nn.silu(c) @ w_ada + b_ada
    sh1, sc1, ga1, sh2, sc2, ga2 = jnp.split(mod[:, None, :], 6, axis=-1)
    h = rmsnorm(x, g1) * (1 + sc1) + sh1
    proj = h @ w_in
    cuts = np.cumsum([2 * M_WIDTH, M_WIDTH, M_WIDTH, 2 * M_HEADS, SB_WIDTH, SB_WIDTH]).tolist()
    qk_pre, v_m, o_m, gates, q_s, k_s, v_s = jnp.split(proj, cuts, axis=-1)
    qk_c, conv_new = causal_conv(qk_pre, conv_buf, conv_w, conv_b)
    qk_c = jax.nn.silu(qk_c)
    q_m = qk_c[..., :M_WIDTH].reshape(B, T, M_HEADS, M_HEAD_DIM)
    k_m = qk_c[..., M_WIDTH:].reshape(B, T, M_HEADS, M_HEAD_DIM) * (M_HEAD_DIM ** -0.5)
    v_m = v_m.reshape(B, T, M_HEADS, M_HEAD_DIM)
    gates = gates.astype(jnp.float32) + b_gate.astype(jnp.float32)
    li = gates[..., :M_HEADS]
    lf = jax.nn.log_sigmoid(gates[..., M_HEADS:])
    h_m, m_state_new = mlstm_seq(q_m, k_m, v_m, li, lf, m_state)
    h_m = rmsnorm(h_m, g_head.reshape(M_HEADS, M_HEAD_DIM)).reshape(B, T, M_WIDTH)
    h_m = (jax.nn.sigmoid(o_m.astype(jnp.float32)) * h_m).astype(x.dtype)
    q_s = q_s.reshape(B, T, SB_HEADS, SB_HEAD_DIM)
    k_s = k_s.reshape(B, T, SB_HEADS, SB_HEAD_DIM)
    v_s = v_s.reshape(B, T, SB_HEADS, SB_HEAD_DIM)
    h_s = sb_mix(q_s, k_s, v_s, sb_bias).reshape(B, T, SB_WIDTH).astype(x.dtype)
    x = x + ga1 * (jnp.concatenate([h_m, h_s], axis=-1) @ w_out)
    h2 = rmsnorm(x, g2) * (1 + sc2) + sh2
    x = x + ga2 * ((jax.nn.silu(h2 @ w_gate) * (h2 @ w_up)) @ w_down)
    return x, conv_new, m_state_new, k_s, v_s


def final_norm(x, c, w_ada_f, b_ada_f, g_f):
    mod = jax.nn.silu(c) @ w_ada_f + b_ada_f
    sh, sc = jnp.split(mod[:, None, :], 2, axis=-1)
    return rmsnorm(x, g_f) * (1 + sc) + sh


def setup_inputs(seed: int = 0) -> dict:
    key = jax.random.key(seed)
    ks = iter(jax.random.split(key, 40))
    def nrm(shape, s):
        return jax.random.normal(next(ks), shape, jnp.float32) * s
    n_pages = PAST_LEN // PAGE_SIZE
    n_used = DEC_BATCH * n_pages
    n_pool = n_used + n_used // 4
    page_table = jax.random.permutation(next(ks), n_pool)[:n_used].reshape(DEC_BATCH, n_pages).astype(jnp.int32)
    d = D_MODEL
    return {
        'x_prompt': nrm((BATCH, SEQ, d), 1.0),
        'x_sample': nrm((DEC_BATCH, DEC_SEQ, d), 1.0),
        'cache_sb_k': nrm((DEPTH, n_pool, PAGE_SIZE, SB_HEADS, SB_HEAD_DIM), 1.0),
        'cache_sb_v': nrm((DEPTH, n_pool, PAGE_SIZE, SB_HEADS, SB_HEAD_DIM), 1.0),
        'state_conv': nrm((DEPTH, DEC_BATCH, CONV_W - 1, 2 * M_WIDTH), 1.0),
        'state_mlstm_C': nrm((DEPTH, DEC_BATCH, M_HEADS, M_HEAD_DIM, M_HEAD_DIM), 0.1),
        'state_mlstm_n': nrm((DEPTH, DEC_BATCH, M_HEADS, M_HEAD_DIM), 0.1),
        'state_mlstm_m': nrm((DEPTH, DEC_BATCH, M_HEADS), 1.0),
        'page_table': page_table,
        'c_prompt': nrm((BATCH, d), 1.0),
        'c_sample': nrm((DEC_BATCH, d), 1.0),
        'w_ada': nrm((DEPTH, d, 6 * d), 0.5 * d ** -0.5),
        'b_ada': nrm((DEPTH, 6 * d), 0.02),
        'norm1_g': 1.0 + nrm((DEPTH, d), 0.02),
        'w_in': nrm((DEPTH, d, IN_COLS), d ** -0.5),
        'b_gate': jnp.concatenate([nrm((DEPTH, M_HEADS), 0.1),
                                   3.0 + 3.0 * jax.random.uniform(next(ks), (DEPTH, M_HEADS), jnp.float32)], axis=-1),
        'sb_bias': SB_BIAS_INIT + nrm((DEPTH, SB_HEADS), 0.1),
        'conv_w': nrm((DEPTH, CONV_W, 2 * M_WIDTH), CONV_W ** -0.5),
        'conv_b': nrm((DEPTH, 2 * M_WIDTH), 0.02),
        'head_norm_g': 1.0 + nrm((DEPTH, M_WIDTH), 0.02),
        'w_out': nrm((DEPTH, MIX_WIDTH, d), MIX_WIDTH ** -0.5),
        'norm2_g': 1.0 + nrm((DEPTH, d), 0.02),
        'w_gate': nrm((DEPTH, d, FFN_HIDDEN), d ** -0.5),
        'w_up': nrm((DEPTH, d, FFN_HIDDEN), d ** -0.5),
        'w_down': nrm((DEPTH, FFN_HIDDEN, d), FFN_HIDDEN ** -0.5),
        'w_ada_f': nrm((d, 2 * d), 0.5 * d ** -0.5),
        'b_ada_f': nrm((2 * d,), 0.02),
        'norm_f_g': 1.0 + nrm((d,), 0.02),
    }


def reference(x_prompt, x_sample, cache_sb_k, cache_sb_v, state_conv, state_mlstm_C, state_mlstm_n,
              state_mlstm_m, page_table, c_prompt, c_sample, w_ada, b_ada, norm1_g, w_in, b_gate, sb_bias,
              conv_w, conv_b, head_norm_g, w_out, norm2_g, w_gate, w_up, w_down, w_ada_f, b_ada_f, norm_f_g):
    n_sb, t_pages = page_table.shape
    past_len = t_pages * cache_sb_k.shape[2]
    xp, xs = x_prompt, x_sample
    bp = x_prompt.shape[0]
    kp, vp, ksm, vsm, cvp, cvs, Cp, Cs, np_, ns, mp, ms = ([] for _ in range(12))
    for l in range(DEPTH):
        lw = (w_ada[l], b_ada[l], norm1_g[l], w_in[l], b_gate[l], sb_bias[l], conv_w[l], conv_b[l],
              head_norm_g[l], w_out[l], norm2_g[l], w_gate[l], w_up[l], w_down[l])
        conv0 = jnp.zeros((bp, CONV_W - 1, 2 * M_WIDTH), xp.dtype)
        st0 = (jnp.zeros((bp, M_HEADS, M_HEAD_DIM, M_HEAD_DIM), jnp.float32),
               jnp.zeros((bp, M_HEADS, M_HEAD_DIM), jnp.float32),
               jnp.zeros((bp, M_HEADS), jnp.float32))
        xp, cv, st, k_new, v_new = trunk_layer(xp, c_prompt, conv0, st0, sb_prompt, *lw)
        kp.append(k_new); vp.append(v_new); cvp.append(cv)
        Cp.append(st[0]); np_.append(st[1]); mp.append(st[2])
        k_past = cache_sb_k[l][page_table].reshape(n_sb, past_len, SB_HEADS, SB_HEAD_DIM)
        v_past = cache_sb_v[l][page_table].reshape(n_sb, past_len, SB_HEADS, SB_HEAD_DIM)
        sts = (state_mlstm_C[l].astype(jnp.float32), state_mlstm_n[l].astype(jnp.float32),
               state_mlstm_m[l].astype(jnp.float32))
        xs, cv, st, k_new, v_new = trunk_layer(xs, c_sample, state_conv[l], sts,
                                               make_sb_sample(k_past, v_past, past_len), *lw)
        ksm.append(k_new); vsm.append(v_new); cvs.append(cv)
        Cs.append(st[0]); ns.append(st[1]); ms.append(st[2])
    y_prompt = final_norm(xp, c_prompt, w_ada_f, b_ada_f, norm_f_g)
    y_sample = final_norm(xs, c_sample, w_ada_f, b_ada_f, norm_f_g)
    return (y_prompt, y_sample, jnp.stack(kp), jnp.stack(vp), jnp.stack(ksm), jnp.stack(vsm),
            jnp.stack(cvp), jnp.stack(cvs), jnp.stack(Cp), jnp.stack(Cs), jnp.stack(np_), jnp.stack(ns),
            jnp.stack(mp), jnp.stack(ms))
```

```python
import functools

import jax
import jax.numpy as jnp
from jax import lax
from jax.experimental import pallas as pl
from jax.experimental.pallas import tpu as pltpu

F32 = jnp.float32
BF16 = jnp.bfloat16

RMS_EPS = 1e-6
M_HEADS = 4
M_HEAD_DIM = 128
M_WIDTH = M_HEADS * M_HEAD_DIM
SB_HEADS = 8
SB_HEAD_DIM = 64
SB_WIDTH = SB_HEADS * SB_HEAD_DIM
CONV_W = 4
CHUNK = 128
LANES = 128
SUBLANES = 8
PAGES_PER_STEP = 8
NEG_BIG = -1e30
VMEM_LIMIT = 56 * 1024 * 1024


def _cparams(sem):
    return pltpu.CompilerParams(dimension_semantics=sem, vmem_limit_bytes=VMEM_LIMIT)


def _log_sigmoid_parts(z):
    l1p = jnp.log(1.0 + jnp.exp(-jnp.abs(z)))
    ls = jnp.minimum(z, 0.0) - l1p
    return ls, ls - z


def _split2(x):
    hi = x.astype(BF16)
    lo = (x - hi.astype(F32)).astype(BF16)
    return hi, lo


def _rms_mod(x, g, sc, sh):
    y = x * lax.rsqrt(jnp.mean(x * x, axis=-1, keepdims=True) + RMS_EPS)
    return (y * g) * (1.0 + sc) + sh


def _dot(a, b):
    return jnp.dot(a, b, preferred_element_type=F32)


def _dot_nt(a, b):
    return lax.dot_general(a, b, (((1,), (1,)), ((), ())), preferred_element_type=F32)


def _ada_kernel(c_ref, w_ref, b_ref, o_ref):
    c = c_ref[...]
    a = (c * jax.nn.sigmoid(c)).astype(BF16)
    o_ref[...] = _dot(a, w_ref[...].astype(BF16)) + b_ref[...]


def _ada(c_all, w, b):
    m, d = c_all.shape
    n = w.shape[1]
    tn = 1024
    return pl.pallas_call(
        _ada_kernel,
        grid=(n // tn,),
        in_specs=[pl.BlockSpec((m, d), lambda j: (0, 0)),
                  pl.BlockSpec((d, tn), lambda j: (0, j)),
                  pl.BlockSpec((1, tn), lambda j: (0, j))],
        out_specs=pl.BlockSpec((m, tn), lambda j: (0, j)),
        out_shape=jax.ShapeDtypeStruct((m, n), F32),
        compiler_params=_cparams(("arbitrary",)),
        name="ada_mod",
    )(c_all, w, b.reshape(1, n))


def _mod_operand(mod, chunk, group, tm, seq, n_sample, d):
    if group == "sample":
        return mod, pl.BlockSpec((n_sample, d), lambda i, *_: (0, chunk))
    mod3 = mod.reshape(mod.shape[0], 1, mod.shape[1])
    return mod3, pl.BlockSpec((None, 1, d), lambda i, *_: (n_sample + (i * tm) // seq, 0, chunk))


def _inproj_kernel(x_ref, g_ref, sc_ref, sh_ref, w_ref,
                   qk_ref, vm_ref, om_ref, gt_ref, q_ref, kf_ref, vf_ref, kb_ref, vb_ref):
    h = _rms_mod(x_ref[...], g_ref[...], sc_ref[...], sh_ref[...]).astype(BF16)

    def seg(lo, hi):
        return _dot(h, w_ref[:, lo:hi])

    o = 0
    qk_ref[...] = seg(o, o + 2 * M_WIDTH); o += 2 * M_WIDTH
    vm_ref[...] = seg(o, o + M_WIDTH).astype(BF16); o += M_WIDTH
    om_ref[...] = seg(o, o + M_WIDTH); o += M_WIDTH
    gt_ref[...] = seg(o, o + LANES); o += LANES
    q_ref[...] = (seg(o, o + SB_WIDTH) * (SB_HEAD_DIM ** -0.5)).astype(BF16); o += SB_WIDTH
    k = seg(o, o + SB_WIDTH); o += SB_WIDTH
    kf_ref[...] = k
    kb_ref[...] = k.astype(BF16)
    v = seg(o, o + SB_WIDTH)
    vf_ref[...] = v
    vb_ref[...] = v.astype(BF16)


def _inproj(x2, mod, g1, w_cat, group, tm, seq, n_sample):
    n, d = x2.shape
    sh_a, sh_s = _mod_operand(mod, 0, group, tm, seq, n_sample, d)
    sc_a, sc_s = _mod_operand(mod, 1, group, tm, seq, n_sample, d)
    row = lambda w: pl.BlockSpec((tm, w), lambda i: (i, 0))
    outs = [(2 * M_WIDTH, F32), (M_WIDTH, BF16), (M_WIDTH, F32), (LANES, F32),
            (SB_WIDTH, BF16), (SB_WIDTH, F32), (SB_WIDTH, F32), (SB_WIDTH, BF16), (SB_WIDTH, BF16)]
    return pl.pallas_call(
        _inproj_kernel,
        grid=(n // tm,),
        in_specs=[row(d), pl.BlockSpec((1, d), lambda i: (0, 0)), sc_s, sh_s,
                  pl.BlockSpec(w_cat.shape, lambda i: (0, 0))],
        out_specs=[row(w) for w, _ in outs],
        out_shape=[jax.ShapeDtypeStruct((n, w), dt) for w, dt in outs],
        compiler_params=_cparams(("arbitrary",)),
        name="in_proj_" + group,
    )(x2, g1, sc_a, sh_a, w_cat)


def _mlstm_kernel(qk_ref, v_ref, o_ref, gt_ref, cs_ref, c0_ref, n0_ref, m0_ref,
                  cw_ref, cb_ref, bg_ref, gh_ref,
                  hm_ref, c_ref, n_ref, m_ref, ubuf, *, valid):
    L = CHUNK
    ci = pl.program_id(1)

    @pl.when(ci == 0)
    def _():
        c_ref[...] = c0_ref[...]
        n_ref[...] = n0_ref[...]
        m_ref[...] = m0_ref[...]
        ubuf[0:SUBLANES, :] = cs_ref[...]

    u = qk_ref[...]
    ubuf[SUBLANES:SUBLANES + L, :] = u
    cw = cw_ref[...]
    acc = cb_ref[...] + ubuf[5:5 + L, :] * cw[0:1, :]
    acc = acc + ubuf[6:6 + L, :] * cw[1:2, :]
    acc = acc + ubuf[7:7 + L, :] * cw[2:3, :]
    acc = acc + u * cw[3:4, :]
    ubuf[0:SUBLANES, :] = ubuf[L:L + SUBLANES, :]
    qkc = acc * jax.nn.sigmoid(acc)

    gf = gt_ref[...] + bg_ref[...]
    lsg, _ = _log_sigmoid_parts(gf)
    li_all = gf
    if valid is not None:
        ok = (ci * L + lax.broadcasted_iota(jnp.int32, (L, LANES), 0)) < valid
        li_all = jnp.where(ok, gf, NEG_BIG)
        lsg = jnp.where(ok, lsg, 0.0)
    row = lax.broadcasted_iota(jnp.int32, (L, L), 0)
    col = lax.broadcasted_iota(jnp.int32, (L, L), 1)
    tri = row >= col
    tril = jnp.where(tri, 1.0, 0.0).astype(BF16)
    hi = lsg.astype(BF16)
    r1 = lsg - hi.astype(F32)
    mid = r1.astype(BF16)
    lo = (r1 - mid.astype(F32)).astype(BF16)
    bcum = _dot(tril, hi) + _dot(tril, mid) + _dot(tril, lo)
    lane = lax.broadcasted_iota(jnp.int32, (L, LANES), 1)
    gmat = jnp.where(lane < M_HEADS, li_all, bcum)
    gmat_t = gmat.T

    for h in range(M_HEADS):
        sl = slice(h * M_HEAD_DIM, (h + 1) * M_HEAD_DIM)
        q = qkc[:, sl]
        k = qkc[:, M_WIDTH + h * M_HEAD_DIM:M_WIDTH + (h + 1) * M_HEAD_DIM] * (M_HEAD_DIM ** -0.5)
        v = v_ref[:, sl]
        li_row = gmat_t[h:h + 1, :]
        b_row = gmat_t[M_HEADS + h:M_HEADS + h + 1, :]
        li_col = gmat[:, h:h + 1]
        b_col = gmat[:, M_HEADS + h:M_HEADS + h + 1]
        m_prev = m_ref[:, h:h + 1]
        dmat = jnp.where(tri, b_col - b_row + li_row, NEG_BIG)
        m_inter = b_col + m_prev
        m_t = jnp.maximum(m_inter, jnp.max(dmat, axis=-1, keepdims=True))
        w_inter = jnp.exp(m_inter - m_t)
        qb = q.astype(BF16)
        kb = k.astype(BF16)
        w_intra = jnp.exp(dmat - m_t) * _dot_nt(qb, kb)
        c_h = c_ref[h]
        n_h = n_ref[h:h + 1, :]
        num = w_inter * _dot(qb, c_h.astype(BF16)) + _dot(w_intra.astype(BF16), v)
        den = (w_inter * jnp.sum(q * n_h, axis=-1, keepdims=True)
               + jnp.sum(w_intra, axis=-1, keepdims=True))
        hh = num / jnp.maximum(jnp.abs(den), jnp.exp(-m_t))
        b_last = b_col[L - 1:L, :]
        m_new = jnp.maximum(b_last + m_prev,
                            jnp.max(b_last - b_row + li_row, axis=-1, keepdims=True))
        g_col = jnp.exp(b_last - b_col + li_col - m_new)
        decay = jnp.exp(b_last + m_prev - m_new)
        kg = k * g_col
        c_ref[h] = decay * c_h + _dot(kg.T.astype(BF16), v)
        n_ref[h:h + 1, :] = decay * n_h + jnp.sum(kg, axis=0, keepdims=True)
        m_ref[:, h:h + 1] = m_new
        hn = hh * lax.rsqrt(jnp.mean(hh * hh, axis=-1, keepdims=True) + RMS_EPS) * gh_ref[:, sl]
        hm_ref[:, sl] = (jax.nn.sigmoid(o_ref[:, sl]) * hn).astype(BF16)


def _mlstm(qk, vm, om, gt, conv_state8, c0, n0, m0, conv_w, conv_b, b_gate_row, g_head, valid, name):
    b, t, _ = qk.shape
    nc = t // CHUNK
    tok = lambda w: pl.BlockSpec((None, CHUNK, w), lambda i, c: (i, c, 0))
    per_b = lambda shape: pl.BlockSpec((None,) + shape, lambda i, c: (i,) + (0,) * len(shape))
    const = lambda shape: pl.BlockSpec(shape, lambda i, c: (0,) * len(shape))
    hd = M_HEAD_DIM
    return pl.pallas_call(
        functools.partial(_mlstm_kernel, valid=valid),
        grid=(b, nc),
        in_specs=[tok(2 * M_WIDTH), tok(M_WIDTH), tok(M_WIDTH), tok(LANES),
                  per_b((SUBLANES, 2 * M_WIDTH)), per_b((M_HEADS, hd, hd)), per_b((M_HEADS, hd)),
                  per_b((1, M_HEADS)),
                  const((CONV_W, 2 * M_WIDTH)), const((1, 2 * M_WIDTH)), const((1, LANES)),
                  const((1, M_WIDTH))],
        out_specs=[tok(M_WIDTH), per_b((M_HEADS, hd, hd)), per_b((M_HEADS, hd)), per_b((1, M_HEADS))],
        out_shape=[jax.ShapeDtypeStruct((b, t, M_WIDTH), BF16),
                   jax.ShapeDtypeStruct((b, M_HEADS, hd, hd), F32),
                   jax.ShapeDtypeStruct((b, M_HEADS, hd), F32),
                   jax.ShapeDtypeStruct((b, 1, M_HEADS), F32)],
        scratch_shapes=[pltpu.VMEM((SUBLANES + CHUNK, 2 * M_WIDTH), F32)],
        compiler_params=_cparams(("arbitrary", "arbitrary")),
        name=name,
    )(qk, vm, om, gt, conv_state8, c0, n0, m0, conv_w, conv_b, b_gate_row, g_head)


def _sbp_kernel(bias_ref, q_ref, k_ref, v_ref, o_ref):
    blk = CHUNK
    hp = pl.program_id(1)
    qi = pl.program_id(2)
    row = lax.broadcasted_iota(jnp.int32, (blk, blk), 0)
    col = lax.broadcasted_iota(jnp.int32, (blk, blk), 1)
    strict = col < row
    uo = jnp.concatenate([jnp.where(row > col, 1.0, 0.0), jnp.ones((blk, blk), F32)], axis=1).astype(BF16)
    uo2 = jnp.concatenate([uo, uo], axis=0)
    heads_per_step = LANES // SB_HEAD_DIM
    for hh in range(heads_per_step):
        sl = slice(hh * SB_HEAD_DIM, (hh + 1) * SB_HEAD_DIM)
        bias = bias_ref[heads_per_step * hp + hh]
        q = q_ref[:, sl]

        def block(j, carry, acc, diag):
            ks = pl.ds(pl.multiple_of(j * blk, blk), blk)
            z = _dot_nt(q, k_ref[ks, sl]) + bias
            ls, lstay = _log_sigmoid_parts(z)
            if diag:
                lstay = jnp.where(strict, lstay, 0.0)
            hi, lo = _split2(lstay)
            r = _dot(jnp.concatenate([hi, lo], axis=1), uo2)
            a = jnp.exp(ls + r[:, :blk] + carry)
            if diag:
                a = jnp.where(strict, a, 0.0)
            acc = acc + _dot(a.astype(BF16), v_ref[ks, sl])
            return carry + r[:, blk:], acc

        carry, acc = block(qi, jnp.zeros((blk, blk), F32), jnp.zeros((blk, SB_HEAD_DIM), F32), True)
        carry, acc = lax.fori_loop(
            0, qi, lambda t, st: block(qi - 1 - t, st[0], st[1], False), (carry, acc))
        o_ref[:, sl] = acc.astype(BF16)


def _sb_prompt(q, k, v, bias):
    b, t, w = q.shape
    nq = t // CHUNK
    return pl.pallas_call(
        _sbp_kernel,
        grid_spec=pltpu.PrefetchScalarGridSpec(
            num_scalar_prefetch=0,
            grid=(b, w // LANES, nq),
            in_specs=[pl.BlockSpec(memory_space=pltpu.SMEM),
                      pl.BlockSpec((None, CHUNK, LANES), lambda i, h, s: (i, s, h)),
                      pl.BlockSpec((None, t, LANES), lambda i, h, s: (i, 0, h)),
                      pl.BlockSpec((None, t, LANES), lambda i, h, s: (i, 0, h))],
            out_specs=pl.BlockSpec((None, CHUNK, LANES), lambda i, h, s: (i, s, h)),
        ),
        out_shape=jax.ShapeDtypeStruct((b, t, w), BF16),
        compiler_params=_cparams(("arbitrary", "arbitrary", "arbitrary")),
        name="sb_prompt",
    )(bias, q, k, v)


def _sbs_kernel(pt_ref, bias_ref, wq_ref, *refs):
    del pt_ref
    G = PAGES_PER_STEP
    k_refs, v_refs = refs[:G], refs[G:2 * G]
    o_ref, carry_ref, z_ref = refs[2 * G:]
    page = k_refs[0].shape[0]
    rows_per_page = page * SB_HEADS // LANES
    nrow = G * rows_per_page

    @pl.when(pl.program_id(1) == 0)
    def _():
        carry_ref[...] = jnp.zeros_like(carry_ref)
        o_ref[...] = jnp.zeros_like(o_ref)

    row = lax.broadcasted_iota(jnp.int32, (LANES, LANES), 0)
    col = lax.broadcasted_iota(jnp.int32, (LANES, LANES), 1)
    eye = jnp.where(row == col, 1.0, 0.0)
    wq = wq_ref[...]
    for g in range(G):
        k2 = k_refs[g][...].reshape(page * SB_HEADS, SB_HEAD_DIM).astype(BF16)
        r = _dot(k2, wq)
        for i in range(rows_per_page):
            z_ref[g * rows_per_page + i:g * rows_per_page + i + 1, :] = jnp.sum(
                r[i * LANES:(i + 1) * LANES, :] * eye, axis=0, keepdims=True)

    ls, lstay = _log_sigmoid_parts(z_ref[...] + bias_ref[...])
    same_head = (row & (SB_HEADS - 1)) == (col & (SB_HEADS - 1))
    later_pos = (row >> 3) > (col >> 3)
    mm = jnp.concatenate([jnp.where(same_head & later_pos, 1.0, 0.0),
                          jnp.where(same_head, 1.0, 0.0)], axis=1).astype(BF16)
    hi, lo = _split2(lstay)
    r = _dot(jnp.concatenate([hi, lo], axis=1), jnp.concatenate([mm, mm], axis=0))
    within, tot = r[:, :LANES], r[:, LANES:]
    rr = lax.broadcasted_iota(jnp.int32, (nrow, nrow), 0)
    rc = lax.broadcasted_iota(jnp.int32, (nrow, nrow), 1)
    later_rows = jnp.where(rc > rr, 1.0, 0.0).astype(BF16)
    thi, tlo = _split2(tot)
    a = jnp.exp(ls + within + _dot(later_rows, thi) + _dot(later_rows, tlo) + carry_ref[...])
    carry_ref[...] = carry_ref[...] + jnp.sum(tot, axis=0, keepdims=True)

    head_mask = (lax.broadcasted_iota(jnp.int32, (SB_HEADS, LANES), 1) & (SB_HEADS - 1)) == \
        lax.broadcasted_iota(jnp.int32, (SB_HEADS, LANES), 0)
    acc = o_ref[...]
    for g in range(G):
        v2 = v_refs[g][...].reshape(page * SB_HEADS, SB_HEAD_DIM).astype(BF16)
        pieces = []
        for i in range(rows_per_page):
            a_row = a[g * rows_per_page + i:g * rows_per_page + i + 1, :]
            pieces.append(jnp.where(head_mask, jnp.broadcast_to(a_row, (SB_HEADS, LANES)), 0.0))
        abd = jnp.concatenate(pieces, axis=1).astype(BF16)
        acc = acc + _dot(abd, v2)
    o_ref[...] = acc


def _sb_sample(layer, q_bf, cache_k, cache_v, page_table, bias):
    b = q_bf.shape[0]
    n_pages = page_table.shape[1]
    page = cache_k.shape[2]
    G = PAGES_PER_STEP
    nstep = n_pages // G
    reps = LANES // SB_HEADS
    wq = jnp.tile(jnp.swapaxes(q_bf.reshape(b, SB_HEADS, SB_HEAD_DIM), 1, 2), (1, 1, reps))
    bias_row = jnp.tile(bias, reps).reshape(1, LANES)

    def page_spec(g):
        return pl.BlockSpec(
            (None, None, page, SB_HEADS, SB_HEAD_DIM),
            lambda i, j, pt: (layer, pt[i, (nstep - 1 - j) * G + g], 0, 0, 0))

    nrow = G * page * SB_HEADS // LANES
    out = pl.pallas_call(
        _sbs_kernel,
        grid_spec=pltpu.PrefetchScalarGridSpec(
            num_scalar_prefetch=1,
            grid=(b, nstep),
            in_specs=[pl.BlockSpec((1, LANES), lambda i, j, pt: (0, 0)),
                      pl.BlockSpec((None, SB_HEAD_DIM, LANES), lambda i, j, pt: (i, 0, 0))]
                     + [page_spec(g) for g in range(G)] * 2,
            out_specs=pl.BlockSpec((None, SB_HEADS, SB_HEAD_DIM), lambda i, j, pt: (i, 0, 0)),
            scratch_shapes=[pltpu.VMEM((1, LANES), F32), pltpu.VMEM((nrow, LANES), F32)],
        ),
        out_shape=jax.ShapeDtypeStruct((b, SB_HEADS, SB_HEAD_DIM), F32),
        compiler_params=_cparams(("arbitrary", "arbitrary")),
        name="sb_sample",
    )(page_table, bias_row, wq, *([cache_k] * G), *([cache_v] * G))
    return out.reshape(b, SB_WIDTH).astype(BF16)


def _ffn_kernel(x_ref, hm_ref, hs_ref, ga1_ref, sh2_ref, sc2_ref, ga2_ref, g2_ref,
                wo_ref, wg_ref, wu_ref, wd_ref, o_ref, x1_ref, h2_ref, acc_ref):
    j = pl.program_id(1)

    @pl.when(j == 0)
    def _():
        attn = _dot(hm_ref[...], wo_ref[0:M_WIDTH, :]) + _dot(hs_ref[...], wo_ref[M_WIDTH:, :])
        x1 = x_ref[...] + ga1_ref[...] * attn
        x1_ref[...] = x1
        h2_ref[...] = _rms_mod(x1, g2_ref[...], sc2_ref[...], sh2_ref[...]).astype(BF16)
        acc_ref[...] = jnp.zeros_like(acc_ref)

    h2 = h2_ref[...]
    gate = _dot(h2, wg_ref[...])
    up = _dot(h2, wu_ref[...])
    act = (gate * jax.nn.sigmoid(gate) * up).astype(BF16)
    acc_ref[...] += _dot(act, wd_ref[...])

    @pl.when(j == pl.num_programs(1) - 1)
    def _():
        o_ref[...] = x1_ref[...] + ga2_ref[...] * acc_ref[...]


def _ffn(x2, hm, hs, mod, g2, wo, wg, wu, wd, group, tm, seq, n_sample):
    n, d = x2.shape
    hid = wg.shape[1]
    th = 256
    ga1_a, ga1_s = _mod_operand(mod, 2, group, tm, seq, n_sample, d)
    sh2_a, sh2_s = _mod_operand(mod, 3, group, tm, seq, n_sample, d)
    sc2_a, sc2_s = _mod_operand(mod, 4, group, tm, seq, n_sample, d)
    ga2_a, ga2_s = _mod_operand(mod, 5, group, tm, seq, n_sample, d)
    row = lambda w: pl.BlockSpec((tm, w), lambda i, j: (i, 0))
    return pl.pallas_call(
        _ffn_kernel,
        grid=(n // tm, hid // th),
        in_specs=[row(d), row(M_WIDTH), row(SB_WIDTH), ga1_s, sh2_s, sc2_s, ga2_s,
                  pl.BlockSpec((1, d), lambda i, j: (0, 0)),
                  pl.BlockSpec(wo.shape, lambda i, j: (0, 0)),
                  pl.BlockSpec((d, th), lambda i, j: (0, j)),
                  pl.BlockSpec((d, th), lambda i, j: (0, j)),
                  pl.BlockSpec((th, d), lambda i, j: (j, 0))],
        out_specs=row(d),
        out_shape=jax.ShapeDtypeStruct((n, d), F32),
        scratch_shapes=[pltpu.VMEM((tm, d), F32), pltpu.VMEM((tm, d), BF16), pltpu.VMEM((tm, d), F32)],
        compiler_params=_cparams(("arbitrary", "arbitrary")),
        name="ffn_" + group,
    )(x2, hm, hs, ga1_a, sh2_a, sc2_a, ga2_a, g2, wo, wg, wu, wd)


def _final_kernel(x_ref, g_ref, sh_ref, sc_ref, o_ref):
    o_ref[...] = _rms_mod(x_ref[...], g_ref[...], sc_ref[...], sh_ref[...])


def _final(x2, mod_f, g_f, group, tm, seq, n_sample):
    n, d = x2.shape
    sh_a, sh_s = _mod_operand(mod_f, 0, group, tm, seq, n_sample, d)
    sc_a, sc_s = _mod_operand(mod_f, 1, group, tm, seq, n_sample, d)
    return pl.pallas_call(
        _final_kernel,
        grid=(n // tm,),
        in_specs=[pl.BlockSpec((tm, d), lambda i: (i, 0)), pl.BlockSpec((1, d), lambda i: (0, 0)),
                  sh_s, sc_s],
        out_specs=pl.BlockSpec((tm, d), lambda i: (i, 0)),
        out_shape=jax.ShapeDtypeStruct((n, d), F32),
        compiler_params=_cparams(("arbitrary",)),
        name="final_norm_" + group,
    )(x2, g_f, sh_a, sc_a)


def kernel(x_prompt, x_sample, cache_sb_k, cache_sb_v, state_conv, state_mlstm_C, state_mlstm_n,
           state_mlstm_m, page_table, c_prompt, c_sample, w_ada, b_ada, norm1_g, w_in, b_gate, sb_bias,
           conv_w, conv_b, head_norm_g, w_out, norm2_g, w_gate, w_up, w_down, w_ada_f, b_ada_f, norm_f_g):
    bp, seq, d = x_prompt.shape
    bs = x_sample.shape[0]
    depth = w_ada.shape[0]
    n_gate = 2 * M_HEADS
    assert x_sample.shape[1] == 1 and seq % CHUNK == 0
    assert page_table.shape[1] % PAGES_PER_STEP == 0

    c_all = jnp.concatenate([c_sample, c_prompt], axis=0)
    xp = x_prompt.reshape(bp * seq, d)
    xs = x_sample.reshape(bs, d)
    tm_p = min(512, seq)
    tm_f = min(1024, seq)
    assert seq % tm_p == 0 and seq % tm_f == 0

    zeros_conv = jnp.zeros((bp, SUBLANES, 2 * M_WIDTH), F32)
    zeros_c = jnp.zeros((bp, M_HEADS, M_HEAD_DIM, M_HEAD_DIM), F32)
    zeros_n = jnp.zeros((bp, M_HEADS, M_HEAD_DIM), F32)
    zeros_m = jnp.zeros((bp, 1, M_HEADS), F32)

    outs = {k: [] for k in ("kp", "vp", "ks", "vs", "cvp", "cvs", "Cp", "Cs", "np", "ns", "mp", "ms")}
    for l in range(depth):
        mod = _ada(c_all, w_ada[l], b_ada[l])
        wi = w_in[l]
        g0 = 4 * M_WIDTH
        w_cat = jnp.concatenate(
            [wi[:, :g0], jnp.pad(wi[:, g0:g0 + n_gate], ((0, 0), (0, LANES - n_gate))), wi[:, g0 + n_gate:]],
            axis=1).astype(BF16)
        wo, wg, wu, wd = (w_out[l].astype(BF16), w_gate[l].astype(BF16), w_up[l].astype(BF16),
                          w_down[l].astype(BF16))
        g1 = norm1_g[l].reshape(1, d)
        g2 = norm2_g[l].reshape(1, d)
        bg_row = jnp.pad(b_gate[l], (0, LANES - n_gate)).reshape(1, LANES)
        cb = conv_b[l].reshape(1, 2 * M_WIDTH)
        gh = head_norm_g[l].reshape(1, M_WIDTH)

        qk, vm, om, gt, q, kf, vf, kb, vb = _inproj(xp, mod, g1, w_cat, "prompt", tm_p, seq, bs)
        r3 = lambda a: a.reshape(bp, seq, a.shape[-1])
        hm, c_new, n_new, m_new = _mlstm(r3(qk), r3(vm), r3(om), r3(gt), zeros_conv, zeros_c, zeros_n,
                                         zeros_m, conv_w[l], cb, bg_row, gh, None, "mlstm_prompt")
        hs = _sb_prompt(r3(q), r3(kb), r3(vb), sb_bias[l])
        xp = _ffn(xp, hm.reshape(bp * seq, M_WIDTH), hs.reshape(bp * seq, SB_WIDTH), mod, g2,
                  wo, wg, wu, wd, "prompt", tm_f, seq, bs)
        outs["kp"].append(kf.reshape(bp, seq, SB_HEADS, SB_HEAD_DIM))
        outs["vp"].append(vf.reshape(bp, seq, SB_HEADS, SB_HEAD_DIM))
        outs["cvp"].append(r3(qk)[:, seq - (CONV_W - 1):, :])
        outs["Cp"].append(c_new)
        outs["np"].append(n_new)
        outs["mp"].append(m_new.reshape(bp, M_HEADS))

        qk, vm, om, gt, q, kf, vf, kb, vb = _inproj(xs, mod, g1, w_cat, "sample", bs, 1, bs)
        pad_t = lambda a: jnp.pad(a[:, None, :], ((0, 0), (0, CHUNK - 1), (0, 0)))
        conv8 = jnp.pad(state_conv[l], ((0, 0), (SUBLANES - (CONV_W - 1), 0), (0, 0)))
        hm, c_new, n_new, m_new = _mlstm(pad_t(qk), pad_t(vm), pad_t(om), pad_t(gt), conv8,
                                         state_mlstm_C[l], state_mlstm_n[l],
                                         state_mlstm_m[l].reshape(bs, 1, M_HEADS),
                                         conv_w[l], cb, bg_row, gh, 1, "mlstm_sample")
        hs = _sb_sample(l, q, cache_sb_k, cache_sb_v, page_table, sb_bias[l])
        xs = _ffn(xs, hm[:, 0, :], hs, mod, g2, wo, wg, wu, wd, "sample", bs, 1, bs)
        outs["ks"].append(kf.reshape(bs, 1, SB_HEADS, SB_HEAD_DIM))
        outs["vs"].append(vf.reshape(bs, 1, SB_HEADS, SB_HEAD_DIM))
        outs["cvs"].append(jnp.concatenate([state_conv[l][:, 1:, :], qk[:, None, :]], axis=1))
        outs["Cs"].append(c_new)
        outs["ns"].append(n_new)
        outs["ms"].append(m_new.reshape(bs, M_HEADS))

    mod_f = _ada(c_all, w_ada_f, b_ada_f)
    gf = norm_f_g.reshape(1, d)
    y_prompt = _final(xp, mod_f, gf, "prompt", tm_f, seq, bs).reshape(bp, seq, d)
    y_sample = _final(xs, mod_f, gf, "sample", bs, 1, bs).reshape(bs, 1, d)
    st = lambda k: jnp.stack(outs[k])
    return (y_prompt, y_sample, st("kp"), st("vp"), st("ks"), st("vs"), st("cvp"), st("cvs"),
            st("Cp"), st("Cs"), st("np"), st("ns"), st("mp"), st("ms"))
```

```python
import functools

import jax
import jax.numpy as jnp
from jax import lax
from jax.experimental import pallas as pl
from jax.experimental.pallas import tpu as pltpu

F32 = jnp.float32
BF16 = jnp.bfloat16

RMS_EPS = 1e-6
M_HEADS = 4
M_HEAD_DIM = 128
M_WIDTH = M_HEADS * M_HEAD_DIM
SB_HEADS = 8
SB_HEAD_DIM = 64
SB_WIDTH = SB_HEADS * SB_HEAD_DIM
CONV_W = 4
CHUNK = 128
LANES = 128
SUBLANES = 8
PAGES_PER_STEP = 16
SBQ = 512
NEG_BIG = -1e30
VMEM_LIMIT = 56 * 1024 * 1024


def _cparams(sem):
    return pltpu.CompilerParams(dimension_semantics=sem, vmem_limit_bytes=VMEM_LIMIT)


def _log_sigmoid_parts(z):
    l1p = jnp.log(1.0 + jnp.exp(-jnp.abs(z)))
    ls = jnp.minimum(z, 0.0) - l1p
    return ls, ls - z


def _split2(x):
    hi = x.astype(BF16)
    lo = (x - hi.astype(F32)).astype(BF16)
    return hi, lo


def _rms_mod(x, g, sc, sh):
    y = x * lax.rsqrt(jnp.mean(x * x, axis=-1, keepdims=True) + RMS_EPS)
    return (y * g) * (1.0 + sc) + sh


def _dot(a, b):
    return jnp.dot(a, b, preferred_element_type=F32)


def _dot_nt(a, b):
    return lax.dot_general(a, b, (((1,), (1,)), ((), ())), preferred_element_type=F32)


def _ada_kernel(c_ref, w_ref, b_ref, o_ref):
    c = c_ref[...]
    a = (c * jax.nn.sigmoid(c)).astype(BF16)
    o_ref[...] = _dot(a, w_ref[...].astype(BF16)) + b_ref[...]


def _ada(c_all, w, b):
    m, d = c_all.shape
    n = w.shape[1]
    tn = 1024
    return pl.pallas_call(
        _ada_kernel,
        grid=(n // tn,),
        in_specs=[pl.BlockSpec((m, d), lambda j: (0, 0)),
                  pl.BlockSpec((d, tn), lambda j: (0, j)),
                  pl.BlockSpec((1, tn), lambda j: (0, j))],
        out_specs=pl.BlockSpec((m, tn), lambda j: (0, j)),
        out_shape=jax.ShapeDtypeStruct((m, n), F32),
        compiler_params=_cparams(("arbitrary",)),
        name="ada_mod",
    )(c_all, w, b.reshape(1, n))


def _mod_operand(mod, chunk, group, tm, seq, n_sample, d):
    if group == "sample":
        return mod, pl.BlockSpec((n_sample, d), lambda i, *_: (0, chunk))
    mod3 = mod.reshape(mod.shape[0], 1, mod.shape[1])
    return mod3, pl.BlockSpec((None, 1, d), lambda i, *_: (n_sample + (i * tm) // seq, 0, chunk))


def _inproj_kernel(x_ref, g_ref, sc_ref, sh_ref, w_ref,
                   qk_ref, vm_ref, om_ref, gt_ref, q_ref, kf_ref, vf_ref, kb_ref, vb_ref):
    h = _rms_mod(x_ref[...], g_ref[...], sc_ref[...], sh_ref[...]).astype(BF16)

    def seg(lo, hi):
        return _dot(h, w_ref[:, lo:hi])

    o = 0
    qk_ref[...] = seg(o, o + 2 * M_WIDTH); o += 2 * M_WIDTH
    vm_ref[...] = seg(o, o + M_WIDTH).astype(BF16); o += M_WIDTH
    om_ref[...] = seg(o, o + M_WIDTH); o += M_WIDTH
    gt_ref[...] = seg(o, o + LANES); o += LANES
    q_ref[...] = (seg(o, o + SB_WIDTH) * (SB_HEAD_DIM ** -0.5)).astype(BF16); o += SB_WIDTH
    k = seg(o, o + SB_WIDTH); o += SB_WIDTH
    kf_ref[...] = k
    kb_ref[...] = k.astype(BF16)
    v = seg(o, o + SB_WIDTH)
    vf_ref[...] = v
    vb_ref[...] = v.astype(BF16)


def _inproj(x2, mod, g1, w_cat, group, tm, seq, n_sample):
    n, d = x2.shape
    sh_a, sh_s = _mod_operand(mod, 0, group, tm, seq, n_sample, d)
    sc_a, sc_s = _mod_operand(mod, 1, group, tm, seq, n_sample, d)
    row = lambda w: pl.BlockSpec((tm, w), lambda i: (i, 0))
    outs = [(2 * M_WIDTH, F32), (M_WIDTH, BF16), (M_WIDTH, F32), (LANES, F32),
            (SB_WIDTH, BF16), (SB_WIDTH, F32), (SB_WIDTH, F32), (SB_WIDTH, BF16), (SB_WIDTH, BF16)]
    return pl.pallas_call(
        _inproj_kernel,
        grid=(n // tm,),
        in_specs=[row(d), pl.BlockSpec((1, d), lambda i: (0, 0)), sc_s, sh_s,
                  pl.BlockSpec(w_cat.shape, lambda i: (0, 0))],
        out_specs=[row(w) for w, _ in outs],
        out_shape=[jax.ShapeDtypeStruct((n, w), dt) for w, dt in outs],
        compiler_params=_cparams(("arbitrary",)),
        name="in_proj_" + group,
    )(x2, g1, sc_a, sh_a, w_cat)


def _mlstm_kernel(qk_ref, v_ref, o_ref, gt_ref, cs_ref, c0_ref, n0_ref, m0_ref,
                  cw_ref, cb_ref, bg_ref, gh_ref,
                  hm_ref, c_ref, n_ref, m_ref, ubuf, *, valid):
    L = CHUNK
    ci = pl.program_id(1)

    @pl.when(ci == 0)
    def _():
        c_ref[...] = c0_ref[...]
        n_ref[...] = n0_ref[...]
        m_ref[...] = m0_ref[...]
        ubuf[0:SUBLANES, :] = cs_ref[...]

    u = qk_ref[...]
    ubuf[SUBLANES:SUBLANES + L, :] = u
    cw = cw_ref[...]
    acc = cb_ref[...] + ubuf[5:5 + L, :] * cw[0:1, :]
    acc = acc + ubuf[6:6 + L, :] * cw[1:2, :]
    acc = acc + ubuf[7:7 + L, :] * cw[2:3, :]
    acc = acc + u * cw[3:4, :]
    ubuf[0:SUBLANES, :] = ubuf[L:L + SUBLANES, :]
    qkc = acc * jax.nn.sigmoid(acc)

    gf = gt_ref[...] + bg_ref[...]
    lsg, _ = _log_sigmoid_parts(gf)
    li_all = gf
    if valid is not None:
        ok = (ci * L + lax.broadcasted_iota(jnp.int32, (L, LANES), 0)) < valid
        li_all = jnp.where(ok, gf, NEG_BIG)
        lsg = jnp.where(ok, lsg, 0.0)
    row = lax.broadcasted_iota(jnp.int32, (L, L), 0)
    col = lax.broadcasted_iota(jnp.int32, (L, L), 1)
    tri = row >= col
    tril = jnp.where(tri, 1.0, 0.0).astype(BF16)
    hi = lsg.astype(BF16)
    r1 = lsg - hi.astype(F32)
    mid = r1.astype(BF16)
    lo = (r1 - mid.astype(F32)).astype(BF16)
    bcum = _dot(tril, hi) + _dot(tril, mid) + _dot(tril, lo)
    lane = lax.broadcasted_iota(jnp.int32, (L, LANES), 1)
    gmat = jnp.where(lane < M_HEADS, li_all, bcum)
    gmat_t = gmat.T

    for h in range(M_HEADS):
        sl = slice(h * M_HEAD_DIM, (h + 1) * M_HEAD_DIM)
        q = qkc[:, sl]
        k = qkc[:, M_WIDTH + h * M_HEAD_DIM:M_WIDTH + (h + 1) * M_HEAD_DIM] * (M_HEAD_DIM ** -0.5)
        v = v_ref[:, sl]
        li_row = gmat_t[h:h + 1, :]
        b_row = gmat_t[M_HEADS + h:M_HEADS + h + 1, :]
        li_col = gmat[:, h:h + 1]
        b_col = gmat[:, M_HEADS + h:M_HEADS + h + 1]
        m_prev = m_ref[:, h:h + 1]
        dmat = jnp.where(tri, b_col - b_row + li_row, NEG_BIG)
        m_inter = b_col + m_prev
        m_t = jnp.maximum(m_inter, jnp.max(dmat, axis=-1, keepdims=True))
        w_inter = jnp.exp(m_inter - m_t)
        qb = q.astype(BF16)
        kb = k.astype(BF16)
        w_intra = jnp.exp(dmat - m_t) * _dot_nt(qb, kb)
        c_h = c_ref[h]
        n_h = n_ref[h:h + 1, :]
        num = w_inter * _dot(qb, c_h.astype(BF16)) + _dot(w_intra.astype(BF16), v)
        den = (w_inter * jnp.sum(q * n_h, axis=-1, keepdims=True)
               + jnp.sum(w_intra, axis=-1, keepdims=True))
        hh = num / jnp.maximum(jnp.abs(den), jnp.exp(-m_t))
        b_last = b_col[L - 1:L, :]
        m_new = jnp.maximum(b_last + m_prev,
                            jnp.max(b_last - b_row + li_row, axis=-1, keepdims=True))
        g_col = jnp.exp(b_last - b_col + li_col - m_new)
        decay = jnp.exp(b_last + m_prev - m_new)
        kg = k * g_col
        c_ref[h] = decay * c_h + _dot(kg.T.astype(BF16), v)
        n_ref[h:h + 1, :] = decay * n_h + jnp.sum(kg, axis=0, keepdims=True)
        m_ref[:, h:h + 1] = m_new
        hn = hh * lax.rsqrt(jnp.mean(hh * hh, axis=-1, keepdims=True) + RMS_EPS) * gh_ref[:, sl]
        hm_ref[:, sl] = (jax.nn.sigmoid(o_ref[:, sl]) * hn).astype(BF16)


def _mlstm(qk, vm, om, gt, conv_state8, c0, n0, m0, conv_w, conv_b, b_gate_row, g_head, valid, name):
    b, t, _ = qk.shape
    nc = t // CHUNK
    tok = lambda w: pl.BlockSpec((None, CHUNK, w), lambda i, c: (i, c, 0))
    per_b = lambda shape: pl.BlockSpec((None,) + shape, lambda i, c: (i,) + (0,) * len(shape))
    const = lambda shape: pl.BlockSpec(shape, lambda i, c: (0,) * len(shape))
    hd = M_HEAD_DIM
    return pl.pallas_call(
        functools.partial(_mlstm_kernel, valid=valid),
        grid=(b, nc),
        in_specs=[tok(2 * M_WIDTH), tok(M_WIDTH), tok(M_WIDTH), tok(LANES),
                  per_b((SUBLANES, 2 * M_WIDTH)), per_b((M_HEADS, hd, hd)), per_b((M_HEADS, hd)),
                  per_b((1, M_HEADS)),
                  const((CONV_W, 2 * M_WIDTH)), const((1, 2 * M_WIDTH)), const((1, LANES)),
                  const((1, M_WIDTH))],
        out_specs=[tok(M_WIDTH), per_b((M_HEADS, hd, hd)), per_b((M_HEADS, hd)), per_b((1, M_HEADS))],
        out_shape=[jax.ShapeDtypeStruct((b, t, M_WIDTH), BF16),
                   jax.ShapeDtypeStruct((b, M_HEADS, hd, hd), F32),
                   jax.ShapeDtypeStruct((b, M_HEADS, hd), F32),
                   jax.ShapeDtypeStruct((b, 1, M_HEADS), F32)],
        scratch_shapes=[pltpu.VMEM((SUBLANES + CHUNK, 2 * M_WIDTH), F32)],
        compiler_params=_cparams(("arbitrary", "arbitrary")),
        name=name,
    )(qk, vm, om, gt, conv_state8, c0, n0, m0, conv_w, conv_b, b_gate_row, g_head)


def _sbp_kernel(bias_ref, q_ref, k_ref, v_ref, o_ref, carry_ref, acc_ref):
    blk = CHUNK
    tq = q_ref.shape[0]
    sub = tq // blk
    qi = pl.program_id(2)
    row = lax.broadcasted_iota(jnp.int32, (blk, blk), 0)
    col = lax.broadcasted_iota(jnp.int32, (blk, blk), 1)
    uo = jnp.concatenate([jnp.where(row > col, 1.0, 0.0), jnp.ones((blk, blk), F32)], axis=1).astype(BF16)
    uo2 = jnp.concatenate([uo, uo], axis=0)
    first_head = col < SB_HEAD_DIM
    bias2 = bias_ref[...]
    carry_ref[...] = jnp.zeros_like(carry_ref)
    acc_ref[...] = jnp.zeros_like(acc_ref)

    def head_pair_rows(x):
        zero = jnp.zeros_like(x)
        return jnp.concatenate([jnp.where(first_head, x, zero), jnp.where(first_head, zero, x)], axis=0)

    def tile(j, r0, diag):
        n = tq - r0
        ks = pl.ds(pl.multiple_of(j * blk, blk), blk)
        z = _dot_nt(q_ref[r0:tq, :], head_pair_rows(k_ref[ks, :])) + bias2
        ls, lstay = _log_sigmoid_parts(z)
        if diag:
            q_pos = qi * tq + r0 + lax.broadcasted_iota(jnp.int32, (n, 2 * blk), 0)
            k_pos = j * blk + (lax.broadcasted_iota(jnp.int32, (n, 2 * blk), 1) & (blk - 1))
            valid = k_pos < q_pos
            lstay = jnp.where(valid, lstay, 0.0)
        hi, lo = _split2(lstay)
        ra = _dot(jnp.concatenate([hi[:, :blk], lo[:, :blk]], axis=1), uo2)
        rb = _dot(jnp.concatenate([hi[:, blk:], lo[:, blk:]], axis=1), uo2)
        later = jnp.concatenate([ra[:, :blk], rb[:, :blk]], axis=1) + carry_ref[r0:tq, :]
        a = jnp.exp(ls + later)
        if diag:
            a = jnp.where(valid, a, 0.0)
        acc_ref[r0:tq, :] += _dot(a.astype(BF16), head_pair_rows(v_ref[ks, :]))
        carry_ref[r0:tq, :] += jnp.concatenate([ra[:, blk:], rb[:, blk:]], axis=1)

    for m in reversed(range(sub)):
        tile(qi * sub + m, m * blk, True)

    def body(t, c):
        tile(qi * sub - 1 - t, 0, False)
        return c

    lax.fori_loop(0, qi * sub, body, 0)
    o_ref[...] = acc_ref[...].astype(BF16)


def _sb_prompt(q, k, v, bias):
    b, t, w = q.shape
    tq = min(SBQ, t)
    pairs = w // LANES
    per_pair = LANES // SB_HEAD_DIM
    bias2 = jnp.repeat(bias.reshape(pairs, per_pair), CHUNK, axis=1).reshape(pairs, 1, per_pair * CHUNK)
    return pl.pallas_call(
        _sbp_kernel,
        grid=(b, pairs, t // tq),
        in_specs=[pl.BlockSpec((None, 1, per_pair * CHUNK), lambda i, h, s: (h, 0, 0)),
                  pl.BlockSpec((None, tq, LANES), lambda i, h, s: (i, s, h)),
                  pl.BlockSpec((None, t, LANES), lambda i, h, s: (i, 0, h)),
                  pl.BlockSpec((None, t, LANES), lambda i, h, s: (i, 0, h))],
        out_specs=pl.BlockSpec((None, tq, LANES), lambda i, h, s: (i, s, h)),
        out_shape=jax.ShapeDtypeStruct((b, t, w), BF16),
        scratch_shapes=[pltpu.VMEM((tq, per_pair * CHUNK), F32), pltpu.VMEM((tq, LANES), F32)],
        compiler_params=_cparams(("arbitrary", "arbitrary", "arbitrary")),
        name="sb_prompt",
    )(bias2, q, k, v)


def _sbs_kernel(pt_ref, bias_ref, qb_ref, *refs):
    del pt_ref
    G = PAGES_PER_STEP
    H = SB_HEADS
    k_refs, v_refs = refs[:G], refs[G:2 * G]
    o_ref, carry_ref, z_ref, a_ref, acc_ref = refs[2 * G:]
    page = k_refs[0].shape[-1]
    j = pl.program_id(1)

    @pl.when(j == 0)
    def _():
        carry_ref[...] = jnp.zeros_like(carry_ref)
        acc_ref[...] = jnp.zeros_like(acc_ref)

    for g in range(G):
        for h in range(H):
            z_ref[g * H + h:g * H + h + 1, :] = jnp.sum(k_refs[g][h] * qb_ref[h], axis=0, keepdims=True)
    ls, lstay = _log_sigmoid_parts(z_ref[...] + bias_ref[...])

    row = lax.broadcasted_iota(jnp.int32, (page, page), 0)
    col = lax.broadcasted_iota(jnp.int32, (page, page), 1)
    uo = jnp.concatenate([jnp.where(row > col, 1.0, 0.0), jnp.ones((page, page), F32)], axis=1).astype(BF16)
    hi, lo = _split2(lstay)
    r = _dot(jnp.concatenate([hi, lo], axis=1), jnp.concatenate([uo, uo], axis=0))
    within, tot = r[:, :page], r[:, page:]
    n = G * H
    rr = lax.broadcasted_iota(jnp.int32, (n, n), 0)
    rc = lax.broadcasted_iota(jnp.int32, (n, n), 1)
    shift = H.bit_length() - 1
    later_page = ((rr & (H - 1)) == (rc & (H - 1))) & ((rc >> shift) > (rr >> shift))
    mx = jnp.where(later_page, 1.0, 0.0).astype(BF16)
    thi, tlo = _split2(tot)
    carry = carry_ref[...]
    a_ref[...] = jnp.exp(ls + within + _dot(mx, thi) + _dot(mx, tlo) + jnp.tile(carry, (G, 1)))
    for g in range(G):
        carry = carry + tot[g * H:(g + 1) * H, :]
    carry_ref[...] = carry

    for h in range(H):
        acc = acc_ref[h]
        for g in range(G):
            acc = acc + v_refs[g][h] * a_ref[g * H + h:g * H + h + 1, :]
        acc_ref[h] = acc

    @pl.when(j == pl.num_programs(1) - 1)
    def _():
        ones = jnp.ones((SUBLANES, page), BF16)
        for h in range(H):
            x = acc_ref[h]
            xh = x.astype(BF16)
            r1 = x - xh.astype(F32)
            xm = r1.astype(BF16)
            xl = (r1 - xm.astype(F32)).astype(BF16)
            s = _dot_nt(ones, xh) + _dot_nt(ones, xm) + _dot_nt(ones, xl)
            o_ref[h:h + 1, :] = s[0:1, :]


def _sb_sample(layer, q_bf, cache_k, cache_v, page_table, bias):
    b = q_bf.shape[0]
    n_pages = page_table.shape[1]
    page = cache_k.shape[2]
    G = PAGES_PER_STEP
    H = SB_HEADS
    nstep = n_pages // G
    kt = jnp.transpose(cache_k, (0, 1, 3, 4, 2))
    vt = jnp.transpose(cache_v, (0, 1, 3, 4, 2))
    qb = jnp.broadcast_to(q_bf.astype(F32).reshape(b, H, SB_HEAD_DIM, 1), (b, H, SB_HEAD_DIM, page))
    bias_rows = jnp.broadcast_to(jnp.tile(bias, G).reshape(G * H, 1), (G * H, page))

    def page_spec(g):
        return pl.BlockSpec(
            (None, None, H, SB_HEAD_DIM, page),
            lambda i, j, pt: (layer, pt[i, (nstep - 1 - j) * G + g], 0, 0, 0))

    out = pl.pallas_call(
        _sbs_kernel,
        grid_spec=pltpu.PrefetchScalarGridSpec(
            num_scalar_prefetch=1,
            grid=(b, nstep),
            in_specs=[pl.BlockSpec((G * H, page), lambda i, j, pt: (0, 0)),
                      pl.BlockSpec((None, H, SB_HEAD_DIM, page), lambda i, j, pt: (i, 0, 0, 0))]
                     + [page_spec(g) for g in range(G)] * 2,
            out_specs=pl.BlockSpec((None, H, SB_HEAD_DIM), lambda i, j, pt: (i, 0, 0)),
            scratch_shapes=[pltpu.VMEM((H, page), F32), pltpu.VMEM((G * H, page), F32),
                            pltpu.VMEM((G * H, page), F32), pltpu.VMEM((H, SB_HEAD_DIM, page), F32)],
        ),
        out_shape=jax.ShapeDtypeStruct((b, H, SB_HEAD_DIM), F32),
        compiler_params=_cparams(("arbitrary", "arbitrary")),
        name="sb_sample",
    )(page_table, bias_rows, qb, *([kt] * G), *([vt] * G))
    return out.reshape(b, SB_WIDTH).astype(BF16)


def _ffn_kernel(x_ref, hm_ref, hs_ref, ga1_ref, sh2_ref, sc2_ref, ga2_ref, g2_ref,
                wo_ref, wg_ref, wu_ref, wd_ref, o_ref, x1_ref, h2_ref, acc_ref):
    j = pl.program_id(1)

    @pl.when(j == 0)
    def _():
        attn = _dot(hm_ref[...], wo_ref[0:M_WIDTH, :]) + _dot(hs_ref[...], wo_ref[M_WIDTH:, :])
        x1 = x_ref[...] + ga1_ref[...] * attn
        x1_ref[...] = x1
        h2_ref[...] = _rms_mod(x1, g2_ref[...], sc2_ref[...], sh2_ref[...]).astype(BF16)
        acc_ref[...] = jnp.zeros_like(acc_ref)

    h2 = h2_ref[...]
    gate = _dot(h2, wg_ref[...])
    up = _dot(h2, wu_ref[...])
    act = (gate * jax.nn.sigmoid(gate) * up).astype(BF16)
    acc_ref[...] += _dot(act, wd_ref[...])

    @pl.when(j == pl.num_programs(1) - 1)
    def _():
        o_ref[...] = x1_ref[...] + ga2_ref[...] * acc_ref[...]


def _ffn(x2, hm, hs, mod, g2, wo, wg, wu, wd, group, tm, seq, n_sample):
    n, d = x2.shape
    hid = wg.shape[1]
    th = 256
    ga1_a, ga1_s = _mod_operand(mod, 2, group, tm, seq, n_sample, d)
    sh2_a, sh2_s = _mod_operand(mod, 3, group, tm, seq, n_sample, d)
    sc2_a, sc2_s = _mod_operand(mod, 4, group, tm, seq, n_sample, d)
    ga2_a, ga2_s = _mod_operand(mod, 5, group, tm, seq, n_sample, d)
    row = lambda w: pl.BlockSpec((tm, w), lambda i, j: (i, 0))
    return pl.pallas_call(
        _ffn_kernel,
        grid=(n // tm, hid // th),
        in_specs=[row(d), row(M_WIDTH), row(SB_WIDTH), ga1_s, sh2_s, sc2_s, ga2_s,
                  pl.BlockSpec((1, d), lambda i, j: (0, 0)),
                  pl.BlockSpec(wo.shape, lambda i, j: (0, 0)),
                  pl.BlockSpec((d, th), lambda i, j: (0, j)),
                  pl.BlockSpec((d, th), lambda i, j: (0, j)),
                  pl.BlockSpec((th, d), lambda i, j: (j, 0))],
        out_specs=row(d),
        out_shape=jax.ShapeDtypeStruct((n, d), F32),
        scratch_shapes=[pltpu.VMEM((tm, d), F32), pltpu.VMEM((tm, d), BF16), pltpu.VMEM((tm, d), F32)],
        compiler_params=_cparams(("arbitrary", "arbitrary")),
        name="ffn_" + group,
    )(x2, hm, hs, ga1_a, sh2_a, sc2_a, ga2_a, g2, wo, wg, wu, wd)


def _final_kernel(x_ref, g_ref, sh_ref, sc_ref, o_ref):
    o_ref[...] = _rms_mod(x_ref[...], g_ref[...], sc_ref[...], sh_ref[...])


def _final(x2, mod_f, g_f, group, tm, seq, n_sample):
    n, d = x2.shape
    sh_a, sh_s = _mod_operand(mod_f, 0, group, tm, seq, n_sample, d)
    sc_a, sc_s = _mod_operand(mod_f, 1, group, tm, seq, n_sample, d)
    return pl.pallas_call(
        _final_kernel,
        grid=(n // tm,),
        in_specs=[pl.BlockSpec((tm, d), lambda i: (i, 0)), pl.BlockSpec((1, d), lambda i: (0, 0)),
                  sh_s, sc_s],
        out_specs=pl.BlockSpec((tm, d), lambda i: (i, 0)),
        out_shape=jax.ShapeDtypeStruct((n, d), F32),
        compiler_params=_cparams(("arbitrary",)),
        name="final_norm_" + group,
    )(x2, g_f, sh_a, sc_a)


def kernel(x_prompt, x_sample, cache_sb_k, cache_sb_v, state_conv, state_mlstm_C, state_mlstm_n,
           state_mlstm_m, page_table, c_prompt, c_sample, w_ada, b_ada, norm1_g, w_in, b_gate, sb_bias,
           conv_w, conv_b, head_norm_g, w_out, norm2_g, w_gate, w_up, w_down, w_ada_f, b_ada_f, norm_f_g):
    bp, seq, d = x_prompt.shape
    bs = x_sample.shape[0]
    depth = w_ada.shape[0]
    n_gate = 2 * M_HEADS
    assert x_sample.shape[1] == 1 and seq % CHUNK == 0
    assert page_table.shape[1] % PAGES_PER_STEP == 0

    c_all = jnp.concatenate([c_sample, c_prompt], axis=0)
    xp = x_prompt.reshape(bp * seq, d)
    xs = x_sample.reshape(bs, d)
    tm_p = min(512, seq)
    tm_f = min(1024, seq)
    assert seq % tm_p == 0 and seq % tm_f == 0

    zeros_conv = jnp.zeros((bp, SUBLANES, 2 * M_WIDTH), F32)
    zeros_c = jnp.zeros((bp, M_HEADS, M_HEAD_DIM, M_HEAD_DIM), F32)
    zeros_n = jnp.zeros((bp, M_HEADS, M_HEAD_DIM), F32)
    zeros_m = jnp.zeros((bp, 1, M_HEADS), F32)

    outs = {k: [] for k in ("kp", "vp", "ks", "vs", "cvp", "cvs", "Cp", "Cs", "np", "ns", "mp", "ms")}
    for l in range(depth):
        mod = _ada(c_all, w_ada[l], b_ada[l])
        wi = w_in[l]
        g0 = 4 * M_WIDTH
        w_cat = jnp.concatenate(
            [wi[:, :g0], jnp.pad(wi[:, g0:g0 + n_gate], ((0, 0), (0, LANES - n_gate))), wi[:, g0 + n_gate:]],
            axis=1).astype(BF16)
        wo, wg, wu, wd = (w_out[l].astype(BF16), w_gate[l].astype(BF16), w_up[l].astype(BF16),
                          w_down[l].astype(BF16))
        g1 = norm1_g[l].reshape(1, d)
        g2 = norm2_g[l].reshape(1, d)
        bg_row = jnp.pad(b_gate[l], (0, LANES - n_gate)).reshape(1, LANES)
        cb = conv_b[l].reshape(1, 2 * M_WIDTH)
        gh = head_norm_g[l].reshape(1, M_WIDTH)

        qk, vm, om, gt, q, kf, vf, kb, vb = _inproj(xp, mod, g1, w_cat, "prompt", tm_p, seq, bs)
        r3 = lambda a: a.reshape(bp, seq, a.shape[-1])
        hm, c_new, n_new, m_new = _mlstm(r3(qk), r3(vm), r3(om), r3(gt), zeros_conv, zeros_c, zeros_n,
                                         zeros_m, conv_w[l], cb, bg_row, gh, None, "mlstm_prompt")
        hs = _sb_prompt(r3(q), r3(kb), r3(vb), sb_bias[l])
        xp = _ffn(xp, hm.reshape(bp * seq, M_WIDTH), hs.reshape(bp * seq, SB_WIDTH), mod, g2,
                  wo, wg, wu, wd, "prompt", tm_f, seq, bs)
        outs["kp"].append(kf.reshape(bp, seq, SB_HEADS, SB_HEAD_DIM))
        outs["vp"].append(vf.reshape(bp, seq, SB_HEADS, SB_HEAD_DIM))
        outs["cvp"].append(r3(qk)[:, seq - (CONV_W - 1):, :])
        outs["Cp"].append(c_new)
        outs["np"].append(n_new)
        outs["mp"].append(m_new.reshape(bp, M_HEADS))

        qk, vm, om, gt, q, kf, vf, kb, vb = _inproj(xs, mod, g1, w_cat, "sample", bs, 1, bs)
        pad_t = lambda a: jnp.pad(a[:, None, :], ((0, 0), (0, CHUNK - 1), (0, 0)))
        conv8 = jnp.pad(state_conv[l], ((0, 0), (SUBLANES - (CONV_W - 1), 0), (0, 0)))
        hm, c_new, n_new, m_new = _mlstm(pad_t(qk), pad_t(vm), pad_t(om), pad_t(gt), conv8,
                                         state_mlstm_C[l], state_mlstm_n[l],
                                         state_mlstm_m[l].reshape(bs, 1, M_HEADS),
                                         conv_w[l], cb, bg_row, gh, 1, "mlstm_sample")
        hs = _sb_sample(l, q, cache_sb_k, cache_sb_v, page_table, sb_bias[l])
        xs = _ffn(xs, hm[:, 0, :], hs, mod, g2, wo, wg, wu, wd, "sample", bs, 1, bs)
        outs["ks"].append(kf.reshape(bs, 1, SB_HEADS, SB_HEAD_DIM))
        outs["vs"].append(vf.reshape(bs, 1, SB_HEADS, SB_HEAD_DIM))
        outs["cvs"].append(jnp.concatenate([state_conv[l][:, 1:, :], qk[:, None, :]], axis=1))
        outs["Cs"].append(c_new)
        outs["ns"].append(n_new)
        outs["ms"].append(m_new.reshape(bs, M_HEADS))

    mod_f = _ada(c_all, w_ada_f, b_ada_f)
    gf = norm_f_g.reshape(1, d)
    y_prompt = _final(xp, mod_f, gf, "prompt", tm_f, seq, bs).reshape(bp, seq, d)
    y_sample = _final(xs, mod_f, gf, "sample", bs, 1, bs).reshape(bs, 1, d)
    st = lambda k: jnp.stack(outs[k])
    return (y_prompt, y_sample, st("kp"), st("vp"), st("ks"), st("vs"), st("cvp"), st("cvs"),
            st("Cp"), st("Cs"), st("np"), st("ns"), st("mp"), st("ms"))
```

```python
import functools

import jax
import jax.numpy as jnp
from jax import lax
from jax.experimental import pallas as pl
from jax.experimental.pallas import tpu as pltpu

F32 = jnp.float32
BF16 = jnp.bfloat16

RMS_EPS = 1e-6
M_HEADS = 4
M_HEAD_DIM = 128
M_WIDTH = M_HEADS * M_HEAD_DIM
SB_HEADS = 8
SB_HEAD_DIM = 64
SB_WIDTH = SB_HEADS * SB_HEAD_DIM
CONV_W = 4
CHUNK = 128
LANES = 128
SUBLANES = 8
PAGES_PER_STEP = 16
SBQ = 512
FFN_CHUNK = 256
MLSTM_SEQS_PER_STEP = 2
NEG_BIG = -1e30
VMEM_LIMIT = 56 * 1024 * 1024


def _cparams(sem):
    return pltpu.CompilerParams(dimension_semantics=sem, vmem_limit_bytes=VMEM_LIMIT)


def _log_sigmoid_parts(z):
    l1p = jnp.log(1.0 + jnp.exp(-jnp.abs(z)))
    ls = jnp.minimum(z, 0.0) - l1p
    return ls, ls - z


def _split2(x):
    hi = x.astype(BF16)
    lo = (x - hi.astype(F32)).astype(BF16)
    return hi, lo


def _rms_mod(x, g, sc, sh):
    y = x * lax.rsqrt(jnp.mean(x * x, axis=-1, keepdims=True) + RMS_EPS)
    return (y * g) * (1.0 + sc) + sh


def _dot(a, b):
    return jnp.dot(a, b, preferred_element_type=F32)


def _dot_nt(a, b):
    return lax.dot_general(a, b, (((1,), (1,)), ((), ())), preferred_element_type=F32)


def _ada_kernel(c_ref, w_ref, b_ref, o_ref):
    c = c_ref[...]
    a = (c * jax.nn.sigmoid(c)).astype(BF16)
    o_ref[...] = _dot(a, w_ref[...].astype(BF16)) + b_ref[...]


def _ada(c_all, w, b):
    m, d = c_all.shape
    n = w.shape[1]
    tn = 1024
    return pl.pallas_call(
        _ada_kernel,
        grid=(n // tn,),
        in_specs=[pl.BlockSpec((m, d), lambda j: (0, 0)),
                  pl.BlockSpec((d, tn), lambda j: (0, j)),
                  pl.BlockSpec((1, tn), lambda j: (0, j))],
        out_specs=pl.BlockSpec((m, tn), lambda j: (0, j)),
        out_shape=jax.ShapeDtypeStruct((m, n), F32),
        compiler_params=_cparams(("arbitrary",)),
        name="ada_mod",
    )(c_all, w, b.reshape(1, n))


def _mod_operand(mod, chunk, group, tm, seq, n_sample, d):
    if group == "sample":
        return mod, pl.BlockSpec((n_sample, d), lambda i, *_: (0, chunk))
    mod3 = mod.reshape(mod.shape[0], 1, mod.shape[1])
    return mod3, pl.BlockSpec((None, 1, d), lambda i, *_: (n_sample + (i * tm) // seq, 0, chunk))


_SEG_QK = (0, 2 * M_WIDTH)
_SEG_VM = (_SEG_QK[1], _SEG_QK[1] + M_WIDTH)
_SEG_OM = (_SEG_VM[1], _SEG_VM[1] + M_WIDTH)
_SEG_Q = (_SEG_OM[1], _SEG_OM[1] + SB_WIDTH)
_SEG_K = (_SEG_Q[1], _SEG_Q[1] + SB_WIDTH)
_SEG_V = (_SEG_K[1], _SEG_K[1] + SB_WIDTH)
_SEG_GATES = (_SEG_V[1], _SEG_V[1] + LANES)


def _pack_w_in(w_in_l):
    n_gate = 2 * M_HEADS
    g0 = 4 * M_WIDTH
    wt = w_in_l.T
    return jnp.concatenate(
        [wt[:g0], wt[g0 + n_gate:], jnp.pad(wt[g0:g0 + n_gate], ((0, LANES - n_gate), (0, 0)))],
        axis=0).astype(BF16)


def _conv_silu_split(acc, qc_ref, kc_ref):
    qkc = acc * jax.nn.sigmoid(acc)
    qc_ref[...] = qkc[:, :M_WIDTH].astype(BF16)
    kc_ref[...] = (qkc[:, M_WIDTH:] * (M_HEAD_DIM ** -0.5)).astype(BF16)


def _inproj_prompt_kernel(x_ref, g_ref, sc_ref, sh_ref, wt_ref, cw_ref, cb_ref,
                          qc_ref, kc_ref, vm_ref, om_ref, gt_ref, q_ref, kb_ref, vb_ref,
                          kt_ref, vt_ref, cs_ref, ubuf, *, tiles_per_seq):
    tm = x_ref.shape[0]
    h = _rms_mod(x_ref[...], g_ref[...], sc_ref[...], sh_ref[...]).astype(BF16)
    seg = lambda s: _dot_nt(h, wt_ref[s[0]:s[1], :])

    @pl.when(pl.program_id(0) % tiles_per_seq == 0)
    def _():
        ubuf[0:SUBLANES, :] = jnp.zeros((SUBLANES, ubuf.shape[1]), F32)

    def conv_silu(c0, c1, dst_ref, d0, scale):
        cols = slice(c0, c1)
        u = _dot_nt(h, wt_ref[c0:c1, :])
        ubuf[SUBLANES:SUBLANES + tm, cols] = u
        acc = cb_ref[:, cols] + ubuf[5:5 + tm, cols] * cw_ref[0:1, cols]
        acc = acc + ubuf[6:6 + tm, cols] * cw_ref[1:2, cols]
        acc = acc + ubuf[7:7 + tm, cols] * cw_ref[2:3, cols]
        acc = acc + u * cw_ref[3:4, cols]
        last = ubuf[tm:tm + SUBLANES, cols]
        ubuf[0:SUBLANES, cols] = last
        cs_ref[:, cols] = last
        y = acc * jax.nn.sigmoid(acc)
        if scale is not None:
            y = y * scale
        dst_ref[:, d0:d0 + (c1 - c0)] = y.astype(BF16)

    half = M_WIDTH // 2
    conv_silu(0, half, qc_ref, 0, None)
    vm_ref[...] = seg(_SEG_VM).astype(BF16)
    conv_silu(half, M_WIDTH, qc_ref, half, None)
    om_ref[...] = seg(_SEG_OM)
    conv_silu(M_WIDTH, M_WIDTH + half, kc_ref, 0, M_HEAD_DIM ** -0.5)
    q_ref[...] = (seg(_SEG_Q) * (SB_HEAD_DIM ** -0.5)).astype(BF16)
    conv_silu(M_WIDTH + half, 2 * M_WIDTH, kc_ref, half, M_HEAD_DIM ** -0.5)
    gt_ref[...] = seg(_SEG_GATES)
    k = seg(_SEG_K)
    kb_ref[...] = k.astype(BF16)
    kt_ref[...] = k.T
    v = seg(_SEG_V)
    vb_ref[...] = v.astype(BF16)
    vt_ref[...] = v.T


def _inproj_prompt(x2, mod, g1, wt, conv_w, conv_b, tm, seq, n_sample):
    n, d = x2.shape
    b = n // seq
    tps = seq // tm
    sh_a, sh_s = _mod_operand(mod, 0, "prompt", tm, seq, n_sample, d)
    sc_a, sc_s = _mod_operand(mod, 1, "prompt", tm, seq, n_sample, d)
    row = lambda w: pl.BlockSpec((tm, w), lambda i: (i, 0))
    const = lambda a: pl.BlockSpec(a.shape, lambda i: (0,) * a.ndim)
    tr = pl.BlockSpec((None, SB_WIDTH, tm), lambda i: (i // tps, 0, i % tps))
    rows = [(M_WIDTH, BF16), (M_WIDTH, BF16), (M_WIDTH, BF16), (M_WIDTH, F32), (LANES, F32),
            (SB_WIDTH, BF16), (SB_WIDTH, BF16), (SB_WIDTH, BF16)]
    return pl.pallas_call(
        functools.partial(_inproj_prompt_kernel, tiles_per_seq=tps),
        grid=(n // tm,),
        in_specs=[row(d), const(g1), sc_s, sh_s, const(wt), const(conv_w), const(conv_b)],
        out_specs=[row(w) for w, _ in rows] + [tr, tr,
                   pl.BlockSpec((None, SUBLANES, 2 * M_WIDTH), lambda i: (i // tps, 0, 0))],
        out_shape=[jax.ShapeDtypeStruct((n, w), dt) for w, dt in rows]
                  + [jax.ShapeDtypeStruct((b, SB_WIDTH, seq), F32)] * 2
                  + [jax.ShapeDtypeStruct((b, SUBLANES, 2 * M_WIDTH), F32)],
        scratch_shapes=[pltpu.VMEM((SUBLANES + tm, 2 * M_WIDTH), F32)],
        compiler_params=_cparams(("arbitrary",)),
        name="in_proj_prompt",
    )(x2, g1, sc_a, sh_a, wt, conv_w, conv_b)


def _inproj_sample_kernel(x_ref, g_ref, sc_ref, sh_ref, wt_ref, cw_ref, cb_ref, s0_ref, s1_ref, s2_ref,
                          qk_ref, qc_ref, kc_ref, vm_ref, om_ref, gt_ref, q_ref, kf_ref, vf_ref):
    h = _rms_mod(x_ref[...], g_ref[...], sc_ref[...], sh_ref[...]).astype(BF16)
    seg = lambda s: _dot_nt(h, wt_ref[s[0]:s[1], :])
    u = seg(_SEG_QK)
    qk_ref[...] = u
    cw = cw_ref[...]
    acc = cb_ref[...] + s0_ref[...] * cw[0:1, :]
    acc = acc + s1_ref[...] * cw[1:2, :]
    acc = acc + s2_ref[...] * cw[2:3, :]
    acc = acc + u * cw[3:4, :]
    _conv_silu_split(acc, qc_ref, kc_ref)
    vm_ref[...] = seg(_SEG_VM).astype(BF16)
    om_ref[...] = seg(_SEG_OM)
    gt_ref[...] = seg(_SEG_GATES)
    q_ref[...] = (seg(_SEG_Q) * (SB_HEAD_DIM ** -0.5)).astype(BF16)
    kf_ref[...] = seg(_SEG_K)
    vf_ref[...] = seg(_SEG_V)


def _inproj_sample(x2, mod, g1, wt, conv_w, conv_b, conv_state):
    n, d = x2.shape
    sh_a, sh_s = _mod_operand(mod, 0, "sample", n, 1, n, d)
    sc_a, sc_s = _mod_operand(mod, 1, "sample", n, 1, n, d)
    full = lambda a: pl.BlockSpec(a.shape, lambda i: (0,) * a.ndim)
    states = [conv_state[:, r, :] for r in range(CONV_W - 1)]
    outs = [(2 * M_WIDTH, F32), (M_WIDTH, BF16), (M_WIDTH, BF16), (M_WIDTH, BF16), (M_WIDTH, F32),
            (LANES, F32), (SB_WIDTH, BF16), (SB_WIDTH, F32), (SB_WIDTH, F32)]
    return pl.pallas_call(
        _inproj_sample_kernel,
        grid=(1,),
        in_specs=[full(x2), full(g1), sc_s, sh_s, full(wt), full(conv_w), full(conv_b)]
                 + [full(s) for s in states],
        out_specs=[pl.BlockSpec((n, w), lambda i: (0, 0)) for w, _ in outs],
        out_shape=[jax.ShapeDtypeStruct((n, w), dt) for w, dt in outs],
        compiler_params=_cparams(("arbitrary",)),
        name="in_proj_sample",
    )(x2, g1, sc_a, sh_a, wt, conv_w, conv_b, *states)


def _mlstm_kernel(q_ref, k_ref, v_ref, o_ref, gt_ref, c0_ref, n0_ref, m0_ref, bg_ref, gh_ref,
                  hm_ref, c_ref, n_ref, m_ref, *, valid):
    L = CHUNK
    nb = q_ref.shape[0]
    ci = pl.program_id(1)

    @pl.when(ci == 0)
    def _():
        c_ref[...] = c0_ref[...]
        n_ref[...] = n0_ref[...]
        m_ref[...] = m0_ref[...]

    row = lax.broadcasted_iota(jnp.int32, (L, L), 0)
    col = lax.broadcasted_iota(jnp.int32, (L, L), 1)
    tri = row >= col
    tril = jnp.where(tri, 1.0, 0.0).astype(BF16)
    lane = lax.broadcasted_iota(jnp.int32, (L, LANES), 1)
    head_lane = lax.broadcasted_iota(jnp.int32, (1, M_HEADS), 1)
    src = lax.broadcasted_iota(jnp.int32, (3 * LANES, 2 * LANES), 0) & (LANES - 1)
    dst = lax.broadcasted_iota(jnp.int32, (3 * LANES, 2 * LANES), 1)
    sel = [jnp.where(src == jnp.where(dst < LANES, M_HEADS + h, h), 1.0, 0.0).astype(BF16)
           for h in range(M_HEADS)]
    n_in = [n_ref[bi] for bi in range(nb)]
    m_in = [m_ref[bi] for bi in range(nb)]
    stores = []

    for bi in range(nb):
        n_rows = []
        m_out = m_in[bi]
        gf = gt_ref[bi] + bg_ref[...]
        lsg, _ = _log_sigmoid_parts(gf)
        li_all = gf
        if valid is not None:
            ok = (ci * L + lax.broadcasted_iota(jnp.int32, (L, LANES), 0)) < valid
            li_all = jnp.where(ok, gf, NEG_BIG)
            lsg = jnp.where(ok, lsg, 0.0)
        hi = lsg.astype(BF16)
        r1 = lsg - hi.astype(F32)
        mid = r1.astype(BF16)
        lo = (r1 - mid.astype(F32)).astype(BF16)
        bcum = _dot(tril, hi) + _dot(tril, mid) + _dot(tril, lo)
        gmat = jnp.where(lane < M_HEADS, li_all, bcum)
        gmat_t = gmat.T
        ghi = gmat.astype(BF16)
        g1 = gmat - ghi.astype(F32)
        gmid = g1.astype(BF16)
        g3 = jnp.concatenate([ghi, gmid, (g1 - gmid.astype(F32)).astype(BF16)], axis=1)

        for h in range(M_HEADS):
            sl = slice(h * M_HEAD_DIM, (h + 1) * M_HEAD_DIM)
            qb = q_ref[bi, :, sl]
            kb = k_ref[bi, :, sl]
            v = v_ref[bi, :, sl]
            li_row = gmat_t[h:h + 1, :]
            b_row = gmat_t[M_HEADS + h:M_HEADS + h + 1, :]
            rep = _dot(g3, sel[h])
            b_col, li_col = rep[:, :LANES], rep[:, LANES:]
            m_prev = jnp.sum(jnp.where(head_lane == h, m_in[bi], 0.0), axis=-1, keepdims=True)
            dmat = jnp.where(tri, b_col - b_row + li_row, NEG_BIG)
            m_inter = b_col + m_prev
            m_t = jnp.maximum(m_inter, jnp.max(dmat, axis=-1, keepdims=True))
            w_inter = jnp.exp(m_inter - m_t)
            w_intra = jnp.exp(dmat - m_t) * _dot_nt(qb, kb)
            c_h = c_ref[bi, h]
            n_h = n_in[bi][h:h + 1, :]
            nhi, nlo = _split2(jnp.broadcast_to(n_h, (M_HEAD_DIM, M_HEAD_DIM)))
            q_dot_n = _dot_nt(qb, nhi) + _dot_nt(qb, nlo)
            num = w_inter * _dot(qb, c_h.astype(BF16)) + _dot(w_intra.astype(BF16), v)
            den = w_inter * q_dot_n + jnp.sum(w_intra, axis=-1, keepdims=True)
            hh = num / jnp.maximum(jnp.abs(den), jnp.exp(-m_t))
            b_last = b_col[L - 1:L, :]
            m_new = jnp.maximum(b_last + m_prev,
                                jnp.max(b_last - b_row + li_row, axis=-1, keepdims=True))
            g_col = jnp.exp(b_last - b_col + li_col - m_new)
            decay = jnp.exp(b_last + m_prev - m_new)
            kg = kb.astype(F32) * g_col
            stores.append((c_ref, (bi, h), decay * c_h + _dot(kg.T.astype(BF16), v)))
            n_rows.append(decay * n_h + jnp.sum(kg, axis=0, keepdims=True))
            m_out = jnp.where(head_lane == h, m_new[:, :M_HEADS], m_out)
            hn = hh * lax.rsqrt(jnp.mean(hh * hh, axis=-1, keepdims=True) + RMS_EPS) * gh_ref[:, sl]
            stores.append((hm_ref, (bi, slice(None), sl), (jax.nn.sigmoid(o_ref[bi, :, sl]) * hn).astype(BF16)))
        stores.append((n_ref, (bi,), jnp.concatenate(n_rows, axis=0)))
        stores.append((m_ref, (bi,), m_out))
    for ref, idx, val in stores:
        ref[idx] = val


def _mlstm(qc, kc, vm, om, gt, c0, n0, m0, b_gate_row, g_head, valid, name):
    b, t, _ = qc.shape
    nc = t // CHUNK
    nb = MLSTM_SEQS_PER_STEP
    tok = lambda w: pl.BlockSpec((nb, CHUNK, w), lambda i, c: (i, c, 0))
    per_b = lambda shape: pl.BlockSpec((nb,) + shape, lambda i, c: (i,) + (0,) * len(shape))
    const = lambda shape: pl.BlockSpec(shape, lambda i, c: (0,) * len(shape))
    hd = M_HEAD_DIM
    return pl.pallas_call(
        functools.partial(_mlstm_kernel, valid=valid),
        grid=(b // nb, nc),
        in_specs=[tok(M_WIDTH), tok(M_WIDTH), tok(M_WIDTH), tok(M_WIDTH), tok(LANES),
                  per_b((M_HEADS, hd, hd)), per_b((M_HEADS, hd)), per_b((1, M_HEADS)),
                  const((1, LANES)), const((1, M_WIDTH))],
        out_specs=[tok(M_WIDTH), per_b((M_HEADS, hd, hd)), per_b((M_HEADS, hd)), per_b((1, M_HEADS))],
        out_shape=[jax.ShapeDtypeStruct((b, t, M_WIDTH), BF16),
                   jax.ShapeDtypeStruct((b, M_HEADS, hd, hd), F32),
                   jax.ShapeDtypeStruct((b, M_HEADS, hd), F32),
                   jax.ShapeDtypeStruct((b, 1, M_HEADS), F32)],
        compiler_params=_cparams(("arbitrary", "arbitrary")),
        name=name,
    )(qc, kc, vm, om, gt, c0, n0, m0, b_gate_row, g_head)


def _sbp_kernel(bias_ref, q_ref, k_ref, v_ref, o_ref, carry_ref, acc_ref):
    blk = CHUNK
    tq = q_ref.shape[0]
    sub = tq // blk
    qi = pl.program_id(2)
    row = lax.broadcasted_iota(jnp.int32, (blk, blk), 0)
    col = lax.broadcasted_iota(jnp.int32, (blk, blk), 1)
    uo = jnp.concatenate([jnp.where(row > col, 1.0, 0.0), jnp.ones((blk, blk), F32)], axis=1).astype(BF16)
    uo2 = jnp.concatenate([uo, uo], axis=0)
    first_head = col < SB_HEAD_DIM
    bias2 = bias_ref[...]
    carry_ref[...] = jnp.zeros_like(carry_ref)
    acc_ref[...] = jnp.zeros_like(acc_ref)

    def head_pair_rows(x):
        zero = jnp.zeros_like(x)
        return jnp.concatenate([jnp.where(first_head, x, zero), jnp.where(first_head, zero, x)], axis=0)

    def tile(j, r0, diag):
        n = tq - r0
        ks = pl.ds(pl.multiple_of(j * blk, blk), blk)
        z = _dot_nt(q_ref[r0:tq, :], head_pair_rows(k_ref[ks, :])) + bias2
        ls, lstay = _log_sigmoid_parts(z)
        if diag:
            q_pos = qi * tq + r0 + lax.broadcasted_iota(jnp.int32, (n, 2 * blk), 0)
            k_pos = j * blk + (lax.broadcasted_iota(jnp.int32, (n, 2 * blk), 1) & (blk - 1))
            valid = k_pos < q_pos
            lstay = jnp.where(valid, lstay, 0.0)
        hi, lo = _split2(lstay)
        ra = _dot(jnp.concatenate([hi[:, :blk], lo[:, :blk]], axis=1), uo2)
        rb = _dot(jnp.concatenate([hi[:, blk:], lo[:, blk:]], axis=1), uo2)
        later = jnp.concatenate([ra[:, :blk], rb[:, :blk]], axis=1) + carry_ref[r0:tq, :]
        a = jnp.exp(ls + later)
        if diag:
            a = jnp.where(valid, a, 0.0)
        acc_ref[r0:tq, :] += _dot(a.astype(BF16), head_pair_rows(v_ref[ks, :]))
        carry_ref[r0:tq, :] += jnp.concatenate([ra[:, blk:], rb[:, blk:]], axis=1)

    for m in reversed(range(sub)):
        tile(qi * sub + m, m * blk, True)

    unroll = 2 if sub % 2 == 0 else 1

    def body(t, c):
        for s in range(unroll):
            tile(qi * sub - 1 - (t * unroll + s), 0, False)
        return c

    lax.fori_loop(0, qi * sub // unroll, body, 0)
    o_ref[...] = acc_ref[...].astype(BF16)


def _sb_prompt(q, k, v, bias):
    b, t, w = q.shape
    tq = min(SBQ, t)
    pairs = w // LANES
    per_pair = LANES // SB_HEAD_DIM
    bias2 = jnp.repeat(bias.reshape(pairs, per_pair), CHUNK, axis=1).reshape(pairs, 1, per_pair * CHUNK)
    return pl.pallas_call(
        _sbp_kernel,
        grid=(b, pairs, t // tq),
        in_specs=[pl.BlockSpec((None, 1, per_pair * CHUNK), lambda i, h, s: (h, 0, 0)),
                  pl.BlockSpec((None, tq, LANES), lambda i, h, s: (i, s, h)),
                  pl.BlockSpec((None, t, LANES), lambda i, h, s: (i, 0, h)),
                  pl.BlockSpec((None, t, LANES), lambda i, h, s: (i, 0, h))],
        out_specs=pl.BlockSpec((None, tq, LANES), lambda i, h, s: (i, s, h)),
        out_shape=jax.ShapeDtypeStruct((b, t, w), BF16),
        scratch_shapes=[pltpu.VMEM((tq, per_pair * CHUNK), F32), pltpu.VMEM((tq, LANES), F32)],
        compiler_params=_cparams(("arbitrary", "arbitrary", "arbitrary")),
        name="sb_prompt",
    )(bias2, q, k, v)


def _sbs_kernel(pt_ref, bias_ref, qb_ref, *refs):
    del pt_ref
    G = PAGES_PER_STEP
    H = SB_HEADS
    k_refs, v_refs = refs[:G], refs[G:2 * G]
    o_ref, carry_ref, z_ref, a_ref, acc_ref = refs[2 * G:]
    page = k_refs[0].shape[-1]
    j = pl.program_id(1)

    @pl.when(j == 0)
    def _():
        carry_ref[...] = jnp.zeros_like(carry_ref)
        acc_ref[...] = jnp.zeros_like(acc_ref)

    for g in range(G):
        for h in range(H):
            z_ref[g * H + h:g * H + h + 1, :] = jnp.sum(k_refs[g][h] * qb_ref[h], axis=0, keepdims=True)
    ls, lstay = _log_sigmoid_parts(z_ref[...] + bias_ref[...])

    row = lax.broadcasted_iota(jnp.int32, (page, page), 0)
    col = lax.broadcasted_iota(jnp.int32, (page, page), 1)
    uo = jnp.concatenate([jnp.where(row > col, 1.0, 0.0), jnp.ones((page, page), F32)], axis=1).astype(BF16)
    hi, lo = _split2(lstay)
    r = _dot(jnp.concatenate([hi, lo], axis=1), jnp.concatenate([uo, uo], axis=0))
    within, tot = r[:, :page], r[:, page:]
    n = G * H
    rr = lax.broadcasted_iota(jnp.int32, (n, n), 0)
    rc = lax.broadcasted_iota(jnp.int32, (n, n), 1)
    shift = H.bit_length() - 1
    later_page = ((rr & (H - 1)) == (rc & (H - 1))) & ((rc >> shift) > (rr >> shift))
    mx = jnp.where(later_page, 1.0, 0.0).astype(BF16)
    thi, tlo = _split2(tot)
    carry = carry_ref[...]
    a_ref[...] = jnp.exp(ls + within + _dot(mx, thi) + _dot(mx, tlo) + jnp.tile(carry, (G, 1)))
    for g in range(G):
        carry = carry + tot[g * H:(g + 1) * H, :]
    carry_ref[...] = carry

    for h in range(H):
        acc = acc_ref[h]
        for g in range(G):
            acc = acc + v_refs[g][h] * a_ref[g * H + h:g * H + h + 1, :]
        acc_ref[h] = acc

    @pl.when(j == pl.num_programs(1) - 1)
    def _():
        ones = jnp.ones((SUBLANES, page), BF16)
        for h in range(H):
            x = acc_ref[h]
            xh = x.astype(BF16)
            r1 = x - xh.astype(F32)
            xm = r1.astype(BF16)
            xl = (r1 - xm.astype(F32)).astype(BF16)
            s = _dot_nt(ones, xh) + _dot_nt(ones, xm) + _dot_nt(ones, xl)
            o_ref[h:h + 1, :] = s[0:1, :]


def _sb_sample(layer, q_bf, cache_k, cache_v, page_table, bias):
    b = q_bf.shape[0]
    n_pages = page_table.shape[1]
    page = cache_k.shape[2]
    G = PAGES_PER_STEP
    H = SB_HEADS
    nstep = n_pages // G
    kt = jnp.transpose(cache_k, (0, 1, 3, 4, 2))
    vt = jnp.transpose(cache_v, (0, 1, 3, 4, 2))
    qb = jnp.broadcast_to(q_bf.astype(F32).reshape(b, H, SB_HEAD_DIM, 1), (b, H, SB_HEAD_DIM, page))
    bias_rows = jnp.broadcast_to(jnp.tile(bias, G).reshape(G * H, 1), (G * H, page))

    def page_spec(g):
        return pl.BlockSpec(
            (None, None, H, SB_HEAD_DIM, page),
            lambda i, j, pt: (layer, pt[i, (nstep - 1 - j) * G + g], 0, 0, 0))

    out = pl.pallas_call(
        _sbs_kernel,
        grid_spec=pltpu.PrefetchScalarGridSpec(
            num_scalar_prefetch=1,
            grid=(b, nstep),
            in_specs=[pl.BlockSpec((G * H, page), lambda i, j, pt: (0, 0)),
                      pl.BlockSpec((None, H, SB_HEAD_DIM, page), lambda i, j, pt: (i, 0, 0, 0))]
                     + [page_spec(g) for g in range(G)] * 2,
            out_specs=pl.BlockSpec((None, H, SB_HEAD_DIM), lambda i, j, pt: (i, 0, 0)),
            scratch_shapes=[pltpu.VMEM((H, page), F32), pltpu.VMEM((G * H, page), F32),
                            pltpu.VMEM((G * H, page), F32), pltpu.VMEM((H, SB_HEAD_DIM, page), F32)],
        ),
        out_shape=jax.ShapeDtypeStruct((b, H, SB_HEAD_DIM), F32),
        compiler_params=_cparams(("arbitrary", "arbitrary")),
        name="sb_sample",
    )(page_table, bias_rows, qb, *([kt] * G), *([vt] * G))
    return out.reshape(b, SB_WIDTH).astype(BF16)


def _ffn_kernel(x_ref, hm_ref, hs_ref, ga1_ref, sh2_ref, sc2_ref, ga2_ref, g2_ref,
                wo_ref, wg_ref, wu_ref, wd_ref, *rest, final):
    if final:
        gf_ref, shf_ref, scf_ref, o_ref, x1_ref, h2_ref, acc_ref = rest
    else:
        o_ref, x1_ref, h2_ref, acc_ref = rest
    j = pl.program_id(1)

    @pl.when(j == 0)
    def _():
        attn = _dot(hm_ref[...], wo_ref[0:M_WIDTH, :]) + _dot(hs_ref[...], wo_ref[M_WIDTH:, :])
        x1 = x_ref[...] + ga1_ref[...] * attn
        x1_ref[...] = x1
        h2_ref[...] = _rms_mod(x1, g2_ref[...], sc2_ref[...], sh2_ref[...]).astype(BF16)
        acc_ref[...] = jnp.zeros_like(acc_ref)

    h2 = h2_ref[...]
    gate = _dot(h2, wg_ref[...])
    up = _dot(h2, wu_ref[...])
    act = (gate * jax.nn.sigmoid(gate) * up).astype(BF16)
    acc_ref[...] += _dot(act, wd_ref[...])

    @pl.when(j == pl.num_programs(1) - 1)
    def _():
        x2 = x1_ref[...] + ga2_ref[...] * acc_ref[...]
        if final:
            x2 = _rms_mod(x2, gf_ref[...], scf_ref[...], shf_ref[...])
        o_ref[...] = x2


def _ffn(x2, hm, hs, mod, g2, wo, wg, wu, wd, group, tm, seq, n_sample, final=None):
    n, d = x2.shape
    hid = wd.shape[0]
    th = FFN_CHUNK
    mods = [_mod_operand(mod, c, group, tm, seq, n_sample, d) for c in (2, 3, 4, 5)]
    row = lambda w: pl.BlockSpec((tm, w), lambda i, j: (i, 0))
    vec = pl.BlockSpec((1, d), lambda i, j: (0, 0))
    operands = [x2, hm, hs] + [a for a, _ in mods] + [g2, wo, wg, wu, wd]
    in_specs = ([row(d), row(M_WIDTH), row(SB_WIDTH)] + [s for _, s in mods]
                + [vec, pl.BlockSpec(wo.shape, lambda i, j: (0, 0)),
                   pl.BlockSpec((d, th), lambda i, j: (0, j)),
                   pl.BlockSpec((d, th), lambda i, j: (0, j)),
                   pl.BlockSpec((th, d), lambda i, j: (j, 0))])
    if final is not None:
        mod_f, g_f = final
        fmods = [_mod_operand(mod_f, c, group, tm, seq, n_sample, d) for c in (0, 1)]
        operands += [g_f] + [a for a, _ in fmods]
        in_specs += [vec] + [s for _, s in fmods]
    return pl.pallas_call(
        functools.partial(_ffn_kernel, final=final is not None),
        grid=(n // tm, hid // th),
        in_specs=in_specs,
        out_specs=row(d),
        out_shape=jax.ShapeDtypeStruct((n, d), F32),
        scratch_shapes=[pltpu.VMEM((tm, d), F32), pltpu.VMEM((tm, d), BF16), pltpu.VMEM((tm, d), F32)],
        compiler_params=_cparams(("arbitrary", "arbitrary")),
        name="ffn_" + group,
    )(*operands)


def kernel(x_prompt, x_sample, cache_sb_k, cache_sb_v, state_conv, state_mlstm_C, state_mlstm_n,
           state_mlstm_m, page_table, c_prompt, c_sample, w_ada, b_ada, norm1_g, w_in, b_gate, sb_bias,
           conv_w, conv_b, head_norm_g, w_out, norm2_g, w_gate, w_up, w_down, w_ada_f, b_ada_f, norm_f_g):
    bp, seq, d = x_prompt.shape
    bs = x_sample.shape[0]
    depth = w_ada.shape[0]
    n_gate = 2 * M_HEADS
    assert x_sample.shape[1] == 1 and seq % CHUNK == 0
    assert page_table.shape[1] % PAGES_PER_STEP == 0

    c_all = jnp.concatenate([c_sample, c_prompt], axis=0)
    xp = x_prompt.reshape(bp * seq, d)
    xs = x_sample.reshape(bs, d)
    tm_p = min(512, seq)
    tm_f = min(1024, seq)
    assert seq % tm_p == 0 and seq % tm_f == 0

    assert bp % MLSTM_SEQS_PER_STEP == 0 and bs % MLSTM_SEQS_PER_STEP == 0
    zeros_c = jnp.zeros((bp, M_HEADS, M_HEAD_DIM, M_HEAD_DIM), F32)
    zeros_n = jnp.zeros((bp, M_HEADS, M_HEAD_DIM), F32)
    zeros_m = jnp.zeros((bp, 1, M_HEADS), F32)

    outs = {k: [] for k in ("kp", "vp", "ks", "vs", "cvp", "cvs", "Cp", "Cs", "np", "ns", "mp", "ms")}
    mod_f = _ada(c_all, w_ada_f, b_ada_f)
    gf = norm_f_g.reshape(1, d)
    for l in range(depth):
        mod = _ada(c_all, w_ada[l], b_ada[l])
        wt = _pack_w_in(w_in[l])
        wo, wg, wu, wd = (w_out[l].astype(BF16), w_gate[l].astype(BF16), w_up[l].astype(BF16),
                          w_down[l].astype(BF16))
        g1 = norm1_g[l].reshape(1, d)
        g2 = norm2_g[l].reshape(1, d)
        bg_row = jnp.pad(b_gate[l], (0, LANES - n_gate)).reshape(1, LANES)
        cb = conv_b[l].reshape(1, 2 * M_WIDTH)
        gh = head_norm_g[l].reshape(1, M_WIDTH)
        final = (mod_f, gf) if l == depth - 1 else None

        qc, kc, vm, om, gt, q, kb, vb, kt, vt, cs = _inproj_prompt(
            xp, mod, g1, wt, conv_w[l], cb, tm_p, seq, bs)
        r3 = lambda a: a.reshape(bp, seq, a.shape[-1])
        hm, c_new, n_new, m_new = _mlstm(r3(qc), r3(kc), r3(vm), r3(om), r3(gt), zeros_c, zeros_n,
                                         zeros_m, bg_row, gh, None, "mlstm_prompt")
        hs = _sb_prompt(r3(q), r3(kb), r3(vb), sb_bias[l])
        xp = _ffn(xp, hm.reshape(bp * seq, M_WIDTH), hs.reshape(bp * seq, SB_WIDTH), mod, g2,
                  wo, wg, wu, wd, "prompt", tm_f, seq, bs, final)
        outs["kp"].append(kt)
        outs["vp"].append(vt)
        outs["cvp"].append(cs[:, SUBLANES - (CONV_W - 1):, :])
        outs["Cp"].append(c_new)
        outs["np"].append(n_new)
        outs["mp"].append(m_new.reshape(bp, M_HEADS))

        qk, qc, kc, vm, om, gt, q, kf, vf = _inproj_sample(xs, mod, g1, wt, conv_w[l], cb, state_conv[l])
        pad_t = lambda a: jnp.pad(a[:, None, :], ((0, 0), (0, CHUNK - 1), (0, 0)))
        hm, c_new, n_new, m_new = _mlstm(pad_t(qc), pad_t(kc), pad_t(vm), pad_t(om), pad_t(gt),
                                         state_mlstm_C[l], state_mlstm_n[l],
                                         state_mlstm_m[l].reshape(bs, 1, M_HEADS),
                                         bg_row, gh, 1, "mlstm_sample")
        hs = _sb_sample(l, q, cache_sb_k, cache_sb_v, page_table, sb_bias[l])
        xs = _ffn(xs, hm[:, 0, :], hs, mod, g2, wo, wg, wu, wd, "sample", bs, 1, bs, final)
        outs["ks"].append(kf.reshape(bs, 1, SB_HEADS, SB_HEAD_DIM))
        outs["vs"].append(vf.reshape(bs, 1, SB_HEADS, SB_HEAD_DIM))
        outs["cvs"].append(jnp.concatenate([state_conv[l][:, 1:, :], qk[:, None, :]], axis=1))
        outs["Cs"].append(c_new)
        outs["ns"].append(n_new)
        outs["ms"].append(m_new.reshape(bs, M_HEADS))

    y_prompt = xp.reshape(bp, seq, d)
    y_sample = xs.reshape(bs, 1, d)
    st = lambda k: jnp.stack(outs[k])
    tok_major = lambda a: jnp.transpose(a.reshape(depth, bp, SB_HEADS, SB_HEAD_DIM, seq), (0, 1, 4, 2, 3))
    return (y_prompt, y_sample, tok_major(st("kp")), tok_major(st("vp")), st("ks"), st("vs"),
            st("cvp"), st("cvs"), st("Cp"), st("Cs"), st("np"), st("ns"), st("mp"), st("ms"))
```

```python
import functools

import jax
import jax.numpy as jnp
from jax import lax
from jax.experimental import pallas as pl
from jax.experimental.pallas import tpu as pltpu

F32 = jnp.float32
BF16 = jnp.bfloat16

RMS_EPS = 1e-6
M_HEADS = 4
M_HEAD_DIM = 128
M_WIDTH = M_HEADS * M_HEAD_DIM
SB_HEADS = 8
SB_HEAD_DIM = 64
SB_WIDTH = SB_HEADS * SB_HEAD_DIM
CONV_W = 4
CHUNK = 128
LANES = 128
SUBLANES = 8
PAGES_PER_STEP = 16
SBQ = 512
FFN_CHUNK = 1408
MLSTM_SEQS_PER_STEP = 4
NEG_BIG = -1e30
VMEM_LIMIT = 56 * 1024 * 1024


def _cparams(sem):
    return pltpu.CompilerParams(dimension_semantics=sem, vmem_limit_bytes=VMEM_LIMIT)


def _log_sigmoid_parts(z):
    l1p = jnp.log(1.0 + jnp.exp(-jnp.abs(z)))
    ls = jnp.minimum(z, 0.0) - l1p
    return ls, ls - z


def _split2(x):
    hi = x.astype(BF16)
    lo = (x - hi.astype(F32)).astype(BF16)
    return hi, lo


def _rms_mod(x, g, sc, sh):
    y = x * lax.rsqrt(jnp.mean(x * x, axis=-1, keepdims=True) + RMS_EPS)
    return (y * g) * (1.0 + sc) + sh


def _dot(a, b):
    return jnp.dot(a, b, preferred_element_type=F32)


def _dot_nt(a, b):
    return lax.dot_general(a, b, (((1,), (1,)), ((), ())), preferred_element_type=F32)


def _ada_kernel(c_ref, w_ref, b_ref, o_ref):
    c = c_ref[...]
    a = (c * jax.nn.sigmoid(c)).astype(BF16)
    o_ref[...] = _dot(a, w_ref[...].astype(BF16)) + b_ref[...]


def _ada(c_all, w, b, layer):
    m, d = c_all.shape
    n = w.shape[2]
    tn = 1024
    return pl.pallas_call(
        _ada_kernel,
        grid=(n // tn,),
        in_specs=[pl.BlockSpec((m, d), lambda j: (0, 0)),
                  pl.BlockSpec((None, d, tn), lambda j: (layer, 0, j)),
                  pl.BlockSpec((None, 1, tn), lambda j: (layer, 0, j))],
        out_specs=pl.BlockSpec((m, tn), lambda j: (0, j)),
        out_shape=jax.ShapeDtypeStruct((m, n), F32),
        compiler_params=_cparams(("arbitrary",)),
        name="ada_mod",
    )(c_all, w, b.reshape(b.shape[0], 1, n))


def _mod_operand(mod, chunk, group, tm, seq, n_sample, d):
    if group == "sample":
        return mod, pl.BlockSpec((n_sample, d), lambda i, *_: (0, chunk))
    mod3 = mod.reshape(mod.shape[0], 1, mod.shape[1])
    return mod3, pl.BlockSpec((None, 1, d), lambda i, *_: (n_sample + (i * tm) // seq, 0, chunk))


_SEG_QK = (0, 2 * M_WIDTH)
_SEG_VM = (_SEG_QK[1], _SEG_QK[1] + M_WIDTH)
_SEG_OM = (_SEG_VM[1], _SEG_VM[1] + M_WIDTH)
_SEG_Q = (_SEG_OM[1], _SEG_OM[1] + SB_WIDTH)
_SEG_K = (_SEG_Q[1], _SEG_Q[1] + SB_WIDTH)
_SEG_V = (_SEG_K[1], _SEG_K[1] + SB_WIDTH)
_SEG_GATES = (_SEG_V[1], _SEG_V[1] + LANES)


def _pack_w_in(w_in_l):
    n_gate = 2 * M_HEADS
    g0 = 4 * M_WIDTH
    wt = w_in_l.T
    return jnp.concatenate(
        [wt[:g0], wt[g0 + n_gate:], jnp.pad(wt[g0:g0 + n_gate], ((0, LANES - n_gate), (0, 0)))],
        axis=0).astype(BF16)


def _conv_silu_split(acc, qc_ref, kc_ref):
    qkc = acc * jax.nn.sigmoid(acc)
    qc_ref[...] = qkc[:, :M_WIDTH].astype(BF16)
    kc_ref[...] = (qkc[:, M_WIDTH:] * (M_HEAD_DIM ** -0.5)).astype(BF16)


def _inproj_prompt_kernel(x_ref, g_ref, sc_ref, sh_ref, wt_ref, cw_ref, cb_ref,
                          qc_ref, kc_ref, vm_ref, om_ref, gt_ref, q_ref, kb_ref, vb_ref,
                          kt_ref, vt_ref, cs_ref, ubuf, *, tiles_per_seq):
    tm = x_ref.shape[0]
    h = _rms_mod(x_ref[...], g_ref[...], sc_ref[...], sh_ref[...]).astype(BF16)
    seg = lambda s: _dot_nt(h, wt_ref[s[0]:s[1], :])

    @pl.when(pl.program_id(0) % tiles_per_seq == 0)
    def _():
        ubuf[0:SUBLANES, :] = jnp.zeros((SUBLANES, ubuf.shape[1]), F32)

    def conv_silu(c0, c1, dst_ref, d0, scale):
        cols = slice(c0, c1)
        u = _dot_nt(h, wt_ref[c0:c1, :])
        ubuf[SUBLANES:SUBLANES + tm, cols] = u
        acc = cb_ref[:, cols] + ubuf[5:5 + tm, cols] * cw_ref[0:1, cols]
        acc = acc + ubuf[6:6 + tm, cols] * cw_ref[1:2, cols]
        acc = acc + ubuf[7:7 + tm, cols] * cw_ref[2:3, cols]
        acc = acc + u * cw_ref[3:4, cols]
        last = ubuf[tm:tm + SUBLANES, cols]
        ubuf[0:SUBLANES, cols] = last
        cs_ref[:, cols] = last
        y = acc * jax.nn.sigmoid(acc)
        if scale is not None:
            y = y * scale
        dst_ref[:, d0:d0 + (c1 - c0)] = y.astype(BF16)

    half = M_WIDTH // 2
    conv_silu(0, half, qc_ref, 0, None)
    vm_ref[...] = seg(_SEG_VM).astype(BF16)
    conv_silu(half, M_WIDTH, qc_ref, half, None)
    om_ref[...] = seg(_SEG_OM)
    conv_silu(M_WIDTH, M_WIDTH + half, kc_ref, 0, M_HEAD_DIM ** -0.5)
    q_ref[...] = (seg(_SEG_Q) * (SB_HEAD_DIM ** -0.5)).astype(BF16)
    conv_silu(M_WIDTH + half, 2 * M_WIDTH, kc_ref, half, M_HEAD_DIM ** -0.5)
    gt_ref[...] = seg(_SEG_GATES)
    k = seg(_SEG_K)
    kb_ref[...] = k.astype(BF16)
    kt_ref[...] = k.T
    v = seg(_SEG_V)
    vb_ref[...] = v.astype(BF16)
    vt_ref[...] = v.T


def _inproj_prompt(x2, mod, g1, wt, conv_w, conv_b, tm, seq, n_sample):
    n, d = x2.shape
    b = n // seq
    tps = seq // tm
    sh_a, sh_s = _mod_operand(mod, 0, "prompt", tm, seq, n_sample, d)
    sc_a, sc_s = _mod_operand(mod, 1, "prompt", tm, seq, n_sample, d)
    row = lambda w: pl.BlockSpec((tm, w), lambda i: (i, 0))
    const = lambda a: pl.BlockSpec(a.shape, lambda i: (0,) * a.ndim)
    tr = pl.BlockSpec((None, SB_WIDTH, tm), lambda i: (i // tps, 0, i % tps))
    rows = [(M_WIDTH, BF16), (M_WIDTH, BF16), (M_WIDTH, BF16), (M_WIDTH, F32), (LANES, F32),
            (SB_WIDTH, BF16), (SB_WIDTH, BF16), (SB_WIDTH, BF16)]
    return pl.pallas_call(
        functools.partial(_inproj_prompt_kernel, tiles_per_seq=tps),
        grid=(n // tm,),
        in_specs=[row(d), const(g1), sc_s, sh_s, const(wt), const(conv_w), const(conv_b)],
        out_specs=[row(w) for w, _ in rows] + [tr, tr,
                   pl.BlockSpec((None, SUBLANES, 2 * M_WIDTH), lambda i: (i // tps, 0, 0))],
        out_shape=[jax.ShapeDtypeStruct((n, w), dt) for w, dt in rows]
                  + [jax.ShapeDtypeStruct((b, SB_WIDTH, seq), F32)] * 2
                  + [jax.ShapeDtypeStruct((b, SUBLANES, 2 * M_WIDTH), F32)],
        scratch_shapes=[pltpu.VMEM((SUBLANES + tm, 2 * M_WIDTH), F32)],
        compiler_params=_cparams(("arbitrary",)),
        name="in_proj_prompt",
    )(x2, g1, sc_a, sh_a, wt, conv_w, conv_b)


def _inproj_sample_kernel(x_ref, g_ref, sc_ref, sh_ref, wt_ref, cw_ref, cb_ref, s0_ref, s1_ref, s2_ref,
                          qk_ref, qc_ref, kc_ref, vm_ref, om_ref, gt_ref, q_ref, kf_ref, vf_ref):
    h = _rms_mod(x_ref[...], g_ref[...], sc_ref[...], sh_ref[...]).astype(BF16)
    seg = lambda s: _dot_nt(h, wt_ref[s[0]:s[1], :])
    u = seg(_SEG_QK)
    qk_ref[...] = u
    cw = cw_ref[...]
    acc = cb_ref[...] + s0_ref[...] * cw[0:1, :]
    acc = acc + s1_ref[...] * cw[1:2, :]
    acc = acc + s2_ref[...] * cw[2:3, :]
    acc = acc + u * cw[3:4, :]
    _conv_silu_split(acc, qc_ref, kc_ref)
    vm_ref[...] = seg(_SEG_VM).astype(BF16)
    om_ref[...] = seg(_SEG_OM)
    gt_ref[...] = seg(_SEG_GATES)
    q_ref[...] = (seg(_SEG_Q) * (SB_HEAD_DIM ** -0.5)).astype(BF16)
    kf_ref[...] = seg(_SEG_K)
    vf_ref[...] = seg(_SEG_V)


def _inproj_sample(x2, mod, g1, wt, conv_w, conv_b, conv_state):
    n, d = x2.shape
    sh_a, sh_s = _mod_operand(mod, 0, "sample", n, 1, n, d)
    sc_a, sc_s = _mod_operand(mod, 1, "sample", n, 1, n, d)
    full = lambda a: pl.BlockSpec(a.shape, lambda i: (0,) * a.ndim)
    states = [conv_state[:, r, :] for r in range(CONV_W - 1)]
    outs = [(2 * M_WIDTH, F32), (M_WIDTH, BF16), (M_WIDTH, BF16), (M_WIDTH, BF16), (M_WIDTH, F32),
            (LANES, F32), (SB_WIDTH, BF16), (SB_WIDTH, F32), (SB_WIDTH, F32)]
    return pl.pallas_call(
        _inproj_sample_kernel,
        grid=(1,),
        in_specs=[full(x2), full(g1), sc_s, sh_s, full(wt), full(conv_w), full(conv_b)]
                 + [full(s) for s in states],
        out_specs=[pl.BlockSpec((n, w), lambda i: (0, 0)) for w, _ in outs],
        out_shape=[jax.ShapeDtypeStruct((n, w), dt) for w, dt in outs],
        compiler_params=_cparams(("arbitrary",)),
        name="in_proj_sample",
    )(x2, g1, sc_a, sh_a, wt, conv_w, conv_b, *states)


def _mlstm_kernel(q_ref, k_ref, v_ref, o_ref, gt_ref, c0_ref, n0_ref, m0_ref, bg_ref, gh_ref,
                  hm_ref, c_ref, n_ref, m_ref, *, valid):
    L = CHUNK
    nb = q_ref.shape[0]
    ci = pl.program_id(1)

    @pl.when(ci == 0)
    def _():
        c_ref[...] = c0_ref[...]
        n_ref[...] = n0_ref[...]
        m_ref[...] = m0_ref[...]

    row = lax.broadcasted_iota(jnp.int32, (L, L), 0)
    col = lax.broadcasted_iota(jnp.int32, (L, L), 1)
    tri = row >= col
    tril = jnp.where(tri, 1.0, 0.0).astype(BF16)
    lane = lax.broadcasted_iota(jnp.int32, (L, LANES), 1)
    head_lane = lax.broadcasted_iota(jnp.int32, (1, M_HEADS), 1)
    src = lax.broadcasted_iota(jnp.int32, (3 * LANES, 2 * LANES), 0) & (LANES - 1)
    dst = lax.broadcasted_iota(jnp.int32, (3 * LANES, 2 * LANES), 1)
    sel = [jnp.where(src == jnp.where(dst < LANES, M_HEADS + h, h), 1.0, 0.0).astype(BF16)
           for h in range(M_HEADS)]
    n_in = [n_ref[bi] for bi in range(nb)]
    m_in = [m_ref[bi] for bi in range(nb)]
    stores = []

    items = [(bi, h) for bi in range(nb) for h in range(M_HEADS)]
    gates = []
    for bi in range(nb):
        gf = gt_ref[bi] + bg_ref[...]
        lsg, _ = _log_sigmoid_parts(gf)
        li_all = gf
        if valid is not None:
            ok = (ci * L + lax.broadcasted_iota(jnp.int32, (L, LANES), 0)) < valid
            li_all = jnp.where(ok, gf, NEG_BIG)
            lsg = jnp.where(ok, lsg, 0.0)
        hi = lsg.astype(BF16)
        r1 = lsg - hi.astype(F32)
        mid = r1.astype(BF16)
        lo = (r1 - mid.astype(F32)).astype(BF16)
        bcum = _dot(tril, hi) + _dot(tril, mid) + _dot(tril, lo)
        gmat = jnp.where(lane < M_HEADS, li_all, bcum)
        gmat_t = gmat.T
        ghi = gmat.astype(BF16)
        g1 = gmat - ghi.astype(F32)
        gmid = g1.astype(BF16)
        g3 = jnp.concatenate([ghi, gmid, (g1 - gmid.astype(F32)).astype(BF16)], axis=1)
        gates.append((gmat_t, g3))

    sl = [slice(h * M_HEAD_DIM, (h + 1) * M_HEAD_DIM) for h in range(M_HEADS)]
    qs = [q_ref[bi, :, sl[h]] for bi, h in items]
    ks = [k_ref[bi, :, sl[h]] for bi, h in items]
    vs = [v_ref[bi, :, sl[h]] for bi, h in items]
    cs = [c_ref[bi, h] for bi, h in items]
    ns = [n_in[bi][h:h + 1, :] for bi, h in items]
    li_row = [gates[bi][0][h:h + 1, :] for bi, h in items]
    b_row = [gates[bi][0][M_HEADS + h:M_HEADS + h + 1, :] for bi, h in items]
    m_prev = [jnp.sum(jnp.where(head_lane == h, m_in[bi], 0.0), axis=-1, keepdims=True) for bi, h in items]
    idx = range(len(items))

    rep = [_dot(gates[bi][1], sel[h]) for bi, h in items]
    b_col = [r[:, :LANES] for r in rep]
    li_col = [r[:, LANES:] for r in rep]
    s_qk = [_dot_nt(qs[i], ks[i]) for i in idx]
    q_c = [_dot(qs[i], cs[i].astype(BF16)) for i in idx]
    q_n = []
    for i in idx:
        nhi, nlo = _split2(jnp.broadcast_to(ns[i], (M_HEAD_DIM, M_HEAD_DIM)))
        q_n.append(_dot_nt(qs[i], nhi) + _dot_nt(qs[i], nlo))

    dmat = [jnp.where(tri, b_col[i] - b_row[i] + li_row[i], NEG_BIG) for i in idx]
    row_max = [jnp.max(dmat[i], axis=-1, keepdims=True) for i in idx]
    m_inter = [b_col[i] + m_prev[i] for i in idx]
    m_t = [jnp.maximum(m_inter[i], row_max[i]) for i in idx]
    w_inter = [jnp.exp(m_inter[i] - m_t[i]) for i in idx]
    w_intra = [jnp.exp(dmat[i] - m_t[i]) * s_qk[i] for i in idx]
    w_sum = [jnp.sum(w_intra[i], axis=-1, keepdims=True) for i in idx]
    w_v = [_dot(w_intra[i].astype(BF16), vs[i]) for i in idx]
    hh = []
    for i in idx:
        num = w_inter[i] * q_c[i] + w_v[i]
        den = w_inter[i] * q_n[i] + w_sum[i]
        hh.append(num / jnp.maximum(jnp.abs(den), jnp.exp(-m_t[i])))
    h_ms = [jnp.mean(hh[i] * hh[i], axis=-1, keepdims=True) for i in idx]
    for i, (bi, h) in enumerate(items):
        hn = hh[i] * lax.rsqrt(h_ms[i] + RMS_EPS) * gh_ref[:, sl[h]]
        stores.append((hm_ref, (bi, slice(None), sl[h]),
                       (jax.nn.sigmoid(o_ref[bi, :, sl[h]]) * hn).astype(BF16)))

    b_last = [b_col[i][L - 1:L, :] for i in idx]
    g_max = [jnp.max(b_last[i] - b_row[i] + li_row[i], axis=-1, keepdims=True) for i in idx]
    m_new = [jnp.maximum(b_last[i] + m_prev[i], g_max[i]) for i in idx]
    kg = [ks[i].astype(F32) * jnp.exp(b_last[i] - b_col[i] + li_col[i] - m_new[i]) for i in idx]
    decay = [jnp.exp(b_last[i] + m_prev[i] - m_new[i]) for i in idx]
    kg_t = [kg[i].T.astype(BF16) for i in idx]
    for i, (bi, h) in enumerate(items):
        stores.append((c_ref, (bi, h), decay[i] * cs[i] + _dot(kg_t[i], vs[i])))
    for bi in range(nb):
        mine = [i for i in idx if items[i][0] == bi]
        n_rows = [decay[i] * ns[i] + jnp.sum(kg[i], axis=0, keepdims=True) for i in mine]
        m_out = m_in[bi]
        for i in mine:
            m_out = jnp.where(head_lane == items[i][1], m_new[i][:, :M_HEADS], m_out)
        stores.append((n_ref, (bi,), jnp.concatenate(n_rows, axis=0)))
        stores.append((m_ref, (bi,), m_out))
    for ref, where, val in stores:
        ref[where] = val


def _mlstm(qc, kc, vm, om, gt, c0, n0, m0, b_gate_row, g_head, valid, name):
    b, t, _ = qc.shape
    nc = t // CHUNK
    nb = MLSTM_SEQS_PER_STEP
    tok = lambda w: pl.BlockSpec((nb, CHUNK, w), lambda i, c: (i, c, 0))
    per_b = lambda shape: pl.BlockSpec((nb,) + shape, lambda i, c: (i,) + (0,) * len(shape))
    const = lambda shape: pl.BlockSpec(shape, lambda i, c: (0,) * len(shape))
    hd = M_HEAD_DIM
    return pl.pallas_call(
        functools.partial(_mlstm_kernel, valid=valid),
        grid=(b // nb, nc),
        in_specs=[tok(M_WIDTH), tok(M_WIDTH), tok(M_WIDTH), tok(M_WIDTH), tok(LANES),
                  per_b((M_HEADS, hd, hd)), per_b((M_HEADS, hd)), per_b((1, M_HEADS)),
                  const((1, LANES)), const((1, M_WIDTH))],
        out_specs=[tok(M_WIDTH), per_b((M_HEADS, hd, hd)), per_b((M_HEADS, hd)), per_b((1, M_HEADS))],
        out_shape=[jax.ShapeDtypeStruct((b, t, M_WIDTH), BF16),
                   jax.ShapeDtypeStruct((b, M_HEADS, hd, hd), F32),
                   jax.ShapeDtypeStruct((b, M_HEADS, hd), F32),
                   jax.ShapeDtypeStruct((b, 1, M_HEADS), F32)],
        compiler_params=_cparams(("arbitrary", "arbitrary")),
        name=name,
    )(qc, kc, vm, om, gt, c0, n0, m0, b_gate_row, g_head)


def _sbp_kernel(bias_ref, q_ref, k_ref, v_ref, o_ref, carry_ref, acc_ref):
    blk = CHUNK
    tq = q_ref.shape[0]
    sub = tq // blk
    qi = pl.program_id(2)
    row = lax.broadcasted_iota(jnp.int32, (blk, blk), 0)
    col = lax.broadcasted_iota(jnp.int32, (blk, blk), 1)
    uo = jnp.concatenate([jnp.where(row > col, 1.0, 0.0), jnp.ones((blk, blk), F32)], axis=1).astype(BF16)
    uo2 = jnp.concatenate([uo, uo], axis=0)
    first_head = col < SB_HEAD_DIM
    bias2 = bias_ref[...]
    carry_ref[...] = jnp.zeros_like(carry_ref)
    acc_ref[...] = jnp.zeros_like(acc_ref)

    def head_pair_rows(x):
        zero = jnp.zeros_like(x)
        return jnp.concatenate([jnp.where(first_head, x, zero), jnp.where(first_head, zero, x)], axis=0)

    def tile(j, r0, diag):
        n = tq - r0
        ks = pl.ds(pl.multiple_of(j * blk, blk), blk)
        z = _dot_nt(q_ref[r0:tq, :], head_pair_rows(k_ref[ks, :])) + bias2
        ls, lstay = _log_sigmoid_parts(z)
        if diag:
            q_pos = qi * tq + r0 + lax.broadcasted_iota(jnp.int32, (n, 2 * blk), 0)
            k_pos = j * blk + (lax.broadcasted_iota(jnp.int32, (n, 2 * blk), 1) & (blk - 1))
            valid = k_pos < q_pos
            lstay = jnp.where(valid, lstay, 0.0)
        hi, lo = _split2(lstay)
        ra = _dot(jnp.concatenate([hi[:, :blk], lo[:, :blk]], axis=1), uo2)
        rb = _dot(jnp.concatenate([hi[:, blk:], lo[:, blk:]], axis=1), uo2)
        later = jnp.concatenate([ra[:, :blk], rb[:, :blk]], axis=1) + carry_ref[r0:tq, :]
        a = jnp.exp(ls + later)
        if diag:
            a = jnp.where(valid, a, 0.0)
        acc_ref[r0:tq, :] += _dot(a.astype(BF16), head_pair_rows(v_ref[ks, :]))
        carry_ref[r0:tq, :] += jnp.concatenate([ra[:, blk:], rb[:, blk:]], axis=1)

    for m in reversed(range(sub)):
        tile(qi * sub + m, m * blk, True)

    unroll = 2 if sub % 2 == 0 else 1

    def body(t, c):
        for s in range(unroll):
            tile(qi * sub - 1 - (t * unroll + s), 0, False)
        return c

    lax.fori_loop(0, qi * sub // unroll, body, 0)
    o_ref[...] = acc_ref[...].astype(BF16)


def _sb_prompt(q, k, v, bias):
    b, t, w = q.shape
    tq = min(SBQ, t)
    pairs = w // LANES
    per_pair = LANES // SB_HEAD_DIM
    bias2 = jnp.repeat(bias.reshape(pairs, per_pair), CHUNK, axis=1).reshape(pairs, 1, per_pair * CHUNK)
    return pl.pallas_call(
        _sbp_kernel,
        grid=(b, pairs, t // tq),
        in_specs=[pl.BlockSpec((None, 1, per_pair * CHUNK), lambda i, h, s: (h, 0, 0)),
                  pl.BlockSpec((None, tq, LANES), lambda i, h, s: (i, s, h)),
                  pl.BlockSpec((None, t, LANES), lambda i, h, s: (i, 0, h)),
                  pl.BlockSpec((None, t, LANES), lambda i, h, s: (i, 0, h))],
        out_specs=pl.BlockSpec((None, tq, LANES), lambda i, h, s: (i, s, h)),
        out_shape=jax.ShapeDtypeStruct((b, t, w), BF16),
        scratch_shapes=[pltpu.VMEM((tq, per_pair * CHUNK), F32), pltpu.VMEM((tq, LANES), F32)],
        compiler_params=_cparams(("arbitrary", "arbitrary", "arbitrary")),
        name="sb_prompt",
    )(bias2, q, k, v)


def _sbs_group(first, bias_ref, qb_ref, k_refs, v_refs, carry_ref, z_ref, a_ref, acc_ref):
    G = len(k_refs)
    H = SB_HEADS
    page = k_refs[0].shape[-1]
    for g in range(G):
        for h in range(H):
            z_ref[g * H + h:g * H + h + 1, :] = jnp.sum(k_refs[g][h] * qb_ref[h], axis=0, keepdims=True)
    ls, lstay = _log_sigmoid_parts(z_ref[...] + bias_ref[...])

    row = lax.broadcasted_iota(jnp.int32, (page, page), 0)
    col = lax.broadcasted_iota(jnp.int32, (page, page), 1)
    uo = jnp.concatenate([jnp.where(row > col, 1.0, 0.0), jnp.ones((page, page), F32)], axis=1).astype(BF16)
    hi, lo = _split2(lstay)
    r = _dot(jnp.concatenate([hi, lo], axis=1), jnp.concatenate([uo, uo], axis=0))
    within, tot = r[:, :page], r[:, page:]
    n = G * H
    rr = lax.broadcasted_iota(jnp.int32, (n, n), 0)
    rc = lax.broadcasted_iota(jnp.int32, (n, n), 1)
    shift = H.bit_length() - 1
    later_page = ((rr & (H - 1)) == (rc & (H - 1))) & ((rc >> shift) > (rr >> shift))
    mx = jnp.where(later_page, 1.0, 0.0).astype(BF16)
    thi, tlo = _split2(tot)
    carry = jnp.where(first, 0.0, carry_ref[...])
    a_ref[...] = jnp.exp(ls + within + _dot(mx, thi) + _dot(mx, tlo) + jnp.tile(carry, (G, 1)))
    for g in range(G):
        carry = carry + tot[g * H:(g + 1) * H, :]
    carry_ref[...] = carry

    for h in range(H):
        acc = jnp.where(first, 0.0, acc_ref[h])
        for g in range(G):
            acc = acc + v_refs[g][h] * a_ref[g * H + h:g * H + h + 1, :]
        acc_ref[h] = acc


def _sbs_finish(acc_ref, o_ref):
    page = acc_ref.shape[-1]
    ones = jnp.ones((SUBLANES, page), BF16)
    for h in range(SB_HEADS):
        x = acc_ref[h]
        xh = x.astype(BF16)
        r1 = x - xh.astype(F32)
        xm = r1.astype(BF16)
        xl = (r1 - xm.astype(F32)).astype(BF16)
        s = _dot_nt(ones, xh) + _dot_nt(ones, xm) + _dot_nt(ones, xl)
        o_ref[h:h + 1, :] = s[0:1, :]


def _sbs_kernel(pt_ref, bias_ref, qb_ref, *refs):
    del pt_ref
    G = PAGES_PER_STEP
    k_refs, v_refs = refs[:G], refs[G:2 * G]
    o_ref, carry_ref, z_ref, a_ref, acc_ref = refs[2 * G:]
    j = pl.program_id(1)
    _sbs_group(j == 0, bias_ref, qb_ref, k_refs, v_refs, carry_ref, z_ref, a_ref, acc_ref)

    @pl.when(j == pl.num_programs(1) - 1)
    def _():
        _sbs_finish(acc_ref, o_ref)


def _sb_sample(layer, q_bf, cache_k, cache_v, page_table, bias):
    b = q_bf.shape[0]
    n_pages = page_table.shape[1]
    page = cache_k.shape[2]
    G = PAGES_PER_STEP
    H = SB_HEADS
    groups = n_pages // G
    kt = jnp.transpose(cache_k, (0, 1, 3, 4, 2))
    vt = jnp.transpose(cache_v, (0, 1, 3, 4, 2))
    qb = jnp.broadcast_to(q_bf.astype(F32).reshape(b, H, SB_HEAD_DIM, 1), (b, H, SB_HEAD_DIM, page))
    bias_rows = jnp.broadcast_to(jnp.tile(bias, G).reshape(G * H, 1), (G * H, page))

    def page_spec(g):
        return pl.BlockSpec(
            (None, None, H, SB_HEAD_DIM, page),
            lambda i, j, pt: (layer, pt[i, (groups - 1 - j) * G + g], 0, 0, 0))

    out = pl.pallas_call(
        _sbs_kernel,
        grid_spec=pltpu.PrefetchScalarGridSpec(
            num_scalar_prefetch=1,
            grid=(b, groups),
            in_specs=[pl.BlockSpec((G * H, page), lambda i, j, pt: (0, 0)),
                      pl.BlockSpec((None, H, SB_HEAD_DIM, page), lambda i, j, pt: (i, 0, 0, 0))]
                     + [page_spec(g) for g in range(G)] * 2,
            out_specs=pl.BlockSpec((None, H, SB_HEAD_DIM), lambda i, j, pt: (i, 0, 0)),
            scratch_shapes=[pltpu.VMEM((H, page), F32), pltpu.VMEM((G * H, page), F32),
                            pltpu.VMEM((G * H, page), F32), pltpu.VMEM((H, SB_HEAD_DIM, page), F32)],
        ),
        out_shape=jax.ShapeDtypeStruct((b, H, SB_HEAD_DIM), F32),
        compiler_params=_cparams(("arbitrary", "arbitrary")),
        name="sb_sample",
    )(page_table, bias_rows, qb, *([kt] * G), *([vt] * G))
    return out.reshape(b, SB_WIDTH).astype(BF16)


def _ffn_kernel(x_ref, hm_ref, hs_ref, ga1_ref, sh2_ref, sc2_ref, ga2_ref, g2_ref,
                wo_ref, wg_ref, wu_ref, wd_ref, *rest, final):
    if final:
        gf_ref, shf_ref, scf_ref, o_ref, h2_ref, acc_ref = rest
    else:
        o_ref, h2_ref, acc_ref = rest
    j = pl.program_id(1)

    @pl.when(j == 0)
    def _():
        attn = _dot(hm_ref[...], wo_ref[0:M_WIDTH, :]) + _dot(hs_ref[...], wo_ref[M_WIDTH:, :])
        x1 = x_ref[...] + ga1_ref[...] * attn
        o_ref[...] = x1
        h2_ref[...] = _rms_mod(x1, g2_ref[...], sc2_ref[...], sh2_ref[...]).astype(BF16)
        acc_ref[...] = jnp.zeros_like(acc_ref)

    h2 = h2_ref[...]
    gate = _dot(h2, wg_ref[...])
    up = _dot(h2, wu_ref[...])
    act = (gate * jax.nn.sigmoid(gate) * up).astype(BF16)
    acc_ref[...] += _dot(act, wd_ref[...])

    @pl.when(j == pl.num_programs(1) - 1)
    def _():
        x2 = o_ref[...] + ga2_ref[...] * acc_ref[...]
        if final:
            x2 = _rms_mod(x2, gf_ref[...], scf_ref[...], shf_ref[...])
        o_ref[...] = x2


def _ffn(x2, hm, hs, mod, g2, wo, wg, wu, wd, group, tm, seq, n_sample, final=None):
    n, d = x2.shape
    hid = wd.shape[0]
    th = FFN_CHUNK if hid % FFN_CHUNK == 0 else hid
    mods = [_mod_operand(mod, c, group, tm, seq, n_sample, d) for c in (2, 3, 4, 5)]
    row = lambda w: pl.BlockSpec((tm, w), lambda i, j: (i, 0))
    vec = pl.BlockSpec((1, d), lambda i, j: (0, 0))
    operands = [x2, hm, hs] + [a for a, _ in mods] + [g2, wo, wg, wu, wd]
    in_specs = ([row(d), row(M_WIDTH), row(SB_WIDTH)] + [s for _, s in mods]
                + [vec, pl.BlockSpec(wo.shape, lambda i, j: (0, 0)),
                   pl.BlockSpec((d, th), lambda i, j: (0, j)),
                   pl.BlockSpec((d, th), lambda i, j: (0, j)),
                   pl.BlockSpec((th, d), lambda i, j: (j, 0))])
    if final is not None:
        mod_f, g_f = final
        fmods = [_mod_operand(mod_f, c, group, tm, seq, n_sample, d) for c in (0, 1)]
        operands += [g_f] + [a for a, _ in fmods]
        in_specs += [vec] + [s for _, s in fmods]
    return pl.pallas_call(
        functools.partial(_ffn_kernel, final=final is not None),
        grid=(n // tm, hid // th),
        in_specs=in_specs,
        out_specs=row(d),
        out_shape=jax.ShapeDtypeStruct((n, d), F32),
        scratch_shapes=[pltpu.VMEM((tm, d), BF16), pltpu.VMEM((tm, d), F32)],
        compiler_params=_cparams(("arbitrary", "arbitrary")),
        name="ffn_" + group,
    )(*operands)


def kernel(x_prompt, x_sample, cache_sb_k, cache_sb_v, state_conv, state_mlstm_C, state_mlstm_n,
           state_mlstm_m, page_table, c_prompt, c_sample, w_ada, b_ada, norm1_g, w_in, b_gate, sb_bias,
           conv_w, conv_b, head_norm_g, w_out, norm2_g, w_gate, w_up, w_down, w_ada_f, b_ada_f, norm_f_g):
    bp, seq, d = x_prompt.shape
    bs = x_sample.shape[0]
    depth = w_ada.shape[0]
    n_gate = 2 * M_HEADS
    assert x_sample.shape[1] == 1 and seq % CHUNK == 0
    assert page_table.shape[1] % PAGES_PER_STEP == 0

    c_all = jnp.concatenate([c_sample, c_prompt], axis=0)
    xp = x_prompt.reshape(bp * seq, d)
    xs = x_sample.reshape(bs, d)
    tm_p = min(512, seq)
    tm_f = min(512, seq)
    assert seq % tm_p == 0 and seq % tm_f == 0

    assert bp % MLSTM_SEQS_PER_STEP == 0 and bs % MLSTM_SEQS_PER_STEP == 0
    zeros_c = jnp.zeros((bp, M_HEADS, M_HEAD_DIM, M_HEAD_DIM), F32)
    zeros_n = jnp.zeros((bp, M_HEADS, M_HEAD_DIM), F32)
    zeros_m = jnp.zeros((bp, 1, M_HEADS), F32)

    outs = {k: [] for k in ("kp", "vp", "ks", "vs", "cvp", "cvs", "Cp", "Cs", "np", "ns", "mp", "ms")}
    mod_f = _ada(c_all, w_ada_f[None], b_ada_f[None], 0)
    gf = norm_f_g.reshape(1, d)
    for l in range(depth):
        mod = _ada(c_all, w_ada, b_ada, l)
        wt = _pack_w_in(w_in[l])
        wo, wg, wu, wd = (w_out[l].astype(BF16), w_gate[l].astype(BF16), w_up[l].astype(BF16),
                          w_down[l].astype(BF16))
        g1 = norm1_g[l].reshape(1, d)
        g2 = norm2_g[l].reshape(1, d)
        bg_row = jnp.pad(b_gate[l], (0, LANES - n_gate)).reshape(1, LANES)
        cb = conv_b[l].reshape(1, 2 * M_WIDTH)
        gh = head_norm_g[l].reshape(1, M_WIDTH)
        final = (mod_f, gf) if l == depth - 1 else None

        qc, kc, vm, om, gt, q, kb, vb, kt, vt, cs = _inproj_prompt(
            xp, mod, g1, wt, conv_w[l], cb, tm_p, seq, bs)
        r3 = lambda a: a.reshape(bp, seq, a.shape[-1])
        hm, c_new, n_new, m_new = _mlstm(r3(qc), r3(kc), r3(vm), r3(om), r3(gt), zeros_c, zeros_n,
                                         zeros_m, bg_row, gh, None, "mlstm_prompt")
        hs = _sb_prompt(r3(q), r3(kb), r3(vb), sb_bias[l])
        xp = _ffn(xp, hm.reshape(bp * seq, M_WIDTH), hs.reshape(bp * seq, SB_WIDTH), mod, g2,
                  wo, wg, wu, wd, "prompt", tm_f, seq, bs, final)
        outs["kp"].append(kt)
        outs["vp"].append(vt)
        outs["cvp"].append(cs[:, SUBLANES - (CONV_W - 1):, :])
        outs["Cp"].append(c_new)
        outs["np"].append(n_new)
        outs["mp"].append(m_new.reshape(bp, M_HEADS))

        qk, qc, kc, vm, om, gt, q, kf, vf = _inproj_sample(xs, mod, g1, wt, conv_w[l], cb, state_conv[l])
        pad_t = lambda a: jnp.pad(a[:, None, :], ((0, 0), (0, CHUNK - 1), (0, 0)))
        hm, c_new, n_new, m_new = _mlstm(pad_t(qc), pad_t(kc), pad_t(vm), pad_t(om), pad_t(gt),
                                         state_mlstm_C[l], state_mlstm_n[l],
                                         state_mlstm_m[l].reshape(bs, 1, M_HEADS),
                                         bg_row, gh, 1, "mlstm_sample")
        hs = _sb_sample(l, q, cache_sb_k, cache_sb_v, page_table, sb_bias[l])
        xs = _ffn(xs, hm[:, 0, :], hs, mod, g2, wo, wg, wu, wd, "sample", bs, 1, bs, final)
        outs["ks"].append(kf.reshape(bs, 1, SB_HEADS, SB_HEAD_DIM))
        outs["vs"].append(vf.reshape(bs, 1, SB_HEADS, SB_HEAD_DIM))
        outs["cvs"].append(jnp.concatenate([state_conv[l][:, 1:, :], qk[:, None, :]], axis=1))
        outs["Cs"].append(c_new)
        outs["ns"].append(n_new)
        outs["ms"].append(m_new.reshape(bs, M_HEADS))

    y_prompt = xp.reshape(bp, seq, d)
    y_sample = xs.reshape(bs, 1, d)
    st = lambda k: jnp.stack(outs[k])
    tok_major = lambda a: jnp.transpose(a.reshape(depth, bp, SB_HEADS, SB_HEAD_DIM, seq), (0, 1, 4, 2, 3))
    return (y_prompt, y_sample, tok_major(st("kp")), tok_major(st("vp")), st("ks"), st("vs"),
            st("cvp"), st("cvs"), st("Cp"), st("Cs"), st("np"), st("ns"), st("mp"), st("ms"))
```

```python
import functools

import jax
import jax.numpy as jnp
from jax import lax
from jax.experimental import pallas as pl
from jax.experimental.pallas import tpu as pltpu

F32 = jnp.float32
BF16 = jnp.bfloat16

RMS_EPS = 1e-6
M_HEADS = 4
M_HEAD_DIM = 128
M_WIDTH = M_HEADS * M_HEAD_DIM
SB_HEADS = 8
SB_HEAD_DIM = 64
SB_WIDTH = SB_HEADS * SB_HEAD_DIM
CONV_W = 4
CHUNK = 128
LANES = 128
SUBLANES = 8
PAGES_PER_STEP = 16
SBQ = 512
MLSTM_SEQS_PER_STEP = 4
NEG_BIG = -1e30
VMEM_LIMIT = 56 * 1024 * 1024


def _cparams(sem):
    return pltpu.CompilerParams(dimension_semantics=sem, vmem_limit_bytes=VMEM_LIMIT)


def _log_sigmoid_parts(z):
    l1p = jnp.log(1.0 + jnp.exp(-jnp.abs(z)))
    ls = jnp.minimum(z, 0.0) - l1p
    return ls, ls - z


def _split2(x):
    hi = x.astype(BF16)
    lo = (x - hi.astype(F32)).astype(BF16)
    return hi, lo


def _rms_mod(x, g, sc, sh):
    y = x * lax.rsqrt(jnp.mean(x * x, axis=-1, keepdims=True) + RMS_EPS)
    return (y * g) * (1.0 + sc) + sh


def _dot(a, b):
    return jnp.dot(a, b, preferred_element_type=F32)


def _dot_nt(a, b):
    return lax.dot_general(a, b, (((1,), (1,)), ((), ())), preferred_element_type=F32)


def _ada_kernel(c_ref, w_ref, b_ref, o_ref):
    c = c_ref[...]
    a = (c * jax.nn.sigmoid(c)).astype(BF16)
    o_ref[...] = _dot(a, w_ref[...].astype(BF16)) + b_ref[...]


def _ada(c_all, w, b, layer):
    m, d = c_all.shape
    n = w.shape[2]
    tn = 1024
    return pl.pallas_call(
        _ada_kernel,
        grid=(n // tn,),
        in_specs=[pl.BlockSpec((m, d), lambda j: (0, 0)),
                  pl.BlockSpec((None, d, tn), lambda j: (layer, 0, j)),
                  pl.BlockSpec((None, 1, tn), lambda j: (layer, 0, j))],
        out_specs=pl.BlockSpec((m, tn), lambda j: (0, j)),
        out_shape=jax.ShapeDtypeStruct((m, n), F32),
        compiler_params=_cparams(("arbitrary",)),
        name="ada_mod",
    )(c_all, w, b.reshape(b.shape[0], 1, n))


def _mod_operand(mod, chunk, group, tm, seq, n_sample, d):
    if group == "sample":
        return mod, pl.BlockSpec((n_sample, d), lambda i, *_: (0, chunk))
    mod3 = mod.reshape(mod.shape[0], 1, mod.shape[1])
    return mod3, pl.BlockSpec((None, 1, d), lambda i, *_: (n_sample + (i * tm) // seq, 0, chunk))


_SEG_QK = (0, 2 * M_WIDTH)
_SEG_VM = (_SEG_QK[1], _SEG_QK[1] + M_WIDTH)
_SEG_OM = (_SEG_VM[1], _SEG_VM[1] + M_WIDTH)
_SEG_Q = (_SEG_OM[1], _SEG_OM[1] + SB_WIDTH)
_SEG_K = (_SEG_Q[1], _SEG_Q[1] + SB_WIDTH)
_SEG_V = (_SEG_K[1], _SEG_K[1] + SB_WIDTH)
_SEG_GATES = (_SEG_V[1], _SEG_V[1] + LANES)


def _pack_w_in(w_in_l):
    n_gate = 2 * M_HEADS
    g0 = 4 * M_WIDTH
    wt = w_in_l.T
    return jnp.concatenate(
        [wt[:g0], wt[g0 + n_gate:], jnp.pad(wt[g0:g0 + n_gate], ((0, LANES - n_gate), (0, 0)))],
        axis=0).astype(BF16)


def _conv_silu_split(acc, qc_ref, kc_ref):
    qkc = acc * jax.nn.sigmoid(acc)
    qc_ref[...] = qkc[:, :M_WIDTH].astype(BF16)
    kc_ref[...] = (qkc[:, M_WIDTH:] * (M_HEAD_DIM ** -0.5)).astype(BF16)


def _inproj_prompt_kernel(x_ref, g_ref, sc_ref, sh_ref, wt_ref, cw_ref, cb_ref,
                          qc_ref, kc_ref, vm_ref, om_ref, gt_ref, q_ref, kb_ref, vb_ref,
                          kt_ref, vt_ref, cs_ref, ubuf, *, tiles_per_seq):
    tm = x_ref.shape[0]
    h = _rms_mod(x_ref[...], g_ref[...], sc_ref[...], sh_ref[...]).astype(BF16)
    seg = lambda s: _dot_nt(h, wt_ref[s[0]:s[1], :])

    @pl.when(pl.program_id(0) % tiles_per_seq == 0)
    def _():
        ubuf[0:SUBLANES, :] = jnp.zeros((SUBLANES, ubuf.shape[1]), F32)

    def conv_silu(c0, c1, dst_ref, d0, scale):
        cols = slice(c0, c1)
        u = _dot_nt(h, wt_ref[c0:c1, :])
        ubuf[SUBLANES:SUBLANES + tm, cols] = u
        acc = cb_ref[:, cols] + ubuf[5:5 + tm, cols] * cw_ref[0:1, cols]
        acc = acc + ubuf[6:6 + tm, cols] * cw_ref[1:2, cols]
        acc = acc + ubuf[7:7 + tm, cols] * cw_ref[2:3, cols]
        acc = acc + u * cw_ref[3:4, cols]
        last = ubuf[tm:tm + SUBLANES, cols]
        ubuf[0:SUBLANES, cols] = last
        cs_ref[:, cols] = last
        y = acc * jax.nn.sigmoid(acc)
        if scale is not None:
            y = y * scale
        dst_ref[:, d0:d0 + (c1 - c0)] = y.astype(BF16)

    half = M_WIDTH // 2
    conv_silu(0, half, qc_ref, 0, None)
    vm_ref[...] = seg(_SEG_VM).astype(BF16)
    conv_silu(half, M_WIDTH, qc_ref, half, None)
    om_ref[...] = seg(_SEG_OM)
    conv_silu(M_WIDTH, M_WIDTH + half, kc_ref, 0, M_HEAD_DIM ** -0.5)
    q_ref[...] = (seg(_SEG_Q) * (SB_HEAD_DIM ** -0.5)).astype(BF16)
    conv_silu(M_WIDTH + half, 2 * M_WIDTH, kc_ref, half, M_HEAD_DIM ** -0.5)
    gt_ref[...] = seg(_SEG_GATES)
    k = seg(_SEG_K)
    kb_ref[...] = k.astype(BF16)
    kt_ref[...] = k.T
    v = seg(_SEG_V)
    vb_ref[...] = v.astype(BF16)
    vt_ref[...] = v.T


def _inproj_prompt(x2, mod, g1, wt, conv_w, conv_b, tm, seq, n_sample):
    n, d = x2.shape
    b = n // seq
    tps = seq // tm
    sh_a, sh_s = _mod_operand(mod, 0, "prompt", tm, seq, n_sample, d)
    sc_a, sc_s = _mod_operand(mod, 1, "prompt", tm, seq, n_sample, d)
    row = lambda w: pl.BlockSpec((tm, w), lambda i: (i, 0))
    const = lambda a: pl.BlockSpec(a.shape, lambda i: (0,) * a.ndim)
    tr = pl.BlockSpec((None, SB_WIDTH, tm), lambda i: (i // tps, 0, i % tps))
    rows = [(M_WIDTH, BF16), (M_WIDTH, BF16), (M_WIDTH, BF16), (M_WIDTH, F32), (LANES, F32),
            (SB_WIDTH, BF16), (SB_WIDTH, BF16), (SB_WIDTH, BF16)]
    return pl.pallas_call(
        functools.partial(_inproj_prompt_kernel, tiles_per_seq=tps),
        grid=(n // tm,),
        in_specs=[row(d), const(g1), sc_s, sh_s,
                  pl.BlockSpec(wt.shape, lambda i: (0, 0), pipeline_mode=pl.Buffered(1)),
                  const(conv_w), const(conv_b)],
        out_specs=[row(w) for w, _ in rows] + [tr, tr,
                   pl.BlockSpec((None, SUBLANES, 2 * M_WIDTH), lambda i: (i // tps, 0, 0))],
        out_shape=[jax.ShapeDtypeStruct((n, w), dt) for w, dt in rows]
                  + [jax.ShapeDtypeStruct((b, SB_WIDTH, seq), F32)] * 2
                  + [jax.ShapeDtypeStruct((b, SUBLANES, 2 * M_WIDTH), F32)],
        scratch_shapes=[pltpu.VMEM((SUBLANES + tm, 2 * M_WIDTH), F32)],
        compiler_params=_cparams(("arbitrary",)),
        name="in_proj_prompt",
    )(x2, g1, sc_a, sh_a, wt, conv_w, conv_b)


def _inproj_sample_kernel(x_ref, g_ref, sc_ref, sh_ref, wt_ref, cw_ref, cb_ref, s0_ref, s1_ref, s2_ref,
                          qk_ref, qc_ref, kc_ref, vm_ref, om_ref, gt_ref, q_ref, kf_ref, vf_ref):
    h = _rms_mod(x_ref[...], g_ref[...], sc_ref[...], sh_ref[...]).astype(BF16)
    seg = lambda s: _dot_nt(h, wt_ref[s[0]:s[1], :])
    u = seg(_SEG_QK)
    qk_ref[...] = u
    cw = cw_ref[...]
    acc = cb_ref[...] + s0_ref[...] * cw[0:1, :]
    acc = acc + s1_ref[...] * cw[1:2, :]
    acc = acc + s2_ref[...] * cw[2:3, :]
    acc = acc + u * cw[3:4, :]
    _conv_silu_split(acc, qc_ref, kc_ref)
    vm_ref[...] = seg(_SEG_VM).astype(BF16)
    om_ref[...] = seg(_SEG_OM)
    gt_ref[...] = seg(_SEG_GATES)
    q_ref[...] = (seg(_SEG_Q) * (SB_HEAD_DIM ** -0.5)).astype(BF16)
    kf_ref[...] = seg(_SEG_K)
    vf_ref[...] = seg(_SEG_V)


def _inproj_sample(x2, mod, g1, wt, conv_w, conv_b, conv_state):
    n, d = x2.shape
    sh_a, sh_s = _mod_operand(mod, 0, "sample", n, 1, n, d)
    sc_a, sc_s = _mod_operand(mod, 1, "sample", n, 1, n, d)
    full = lambda a: pl.BlockSpec(a.shape, lambda i: (0,) * a.ndim)
    states = [conv_state[:, r, :] for r in range(CONV_W - 1)]
    outs = [(2 * M_WIDTH, F32), (M_WIDTH, BF16), (M_WIDTH, BF16), (M_WIDTH, BF16), (M_WIDTH, F32),
            (LANES, F32), (SB_WIDTH, BF16), (SB_WIDTH, F32), (SB_WIDTH, F32)]
    return pl.pallas_call(
        _inproj_sample_kernel,
        grid=(1,),
        in_specs=[full(x2), full(g1), sc_s, sh_s, full(wt), full(conv_w), full(conv_b)]
                 + [full(s) for s in states],
        out_specs=[pl.BlockSpec((n, w), lambda i: (0, 0)) for w, _ in outs],
        out_shape=[jax.ShapeDtypeStruct((n, w), dt) for w, dt in outs],
        compiler_params=_cparams(("arbitrary",)),
        name="in_proj_sample",
    )(x2, g1, sc_a, sh_a, wt, conv_w, conv_b, *states)


def _mlstm_kernel(q_ref, k_ref, v_ref, o_ref, gt_ref, c0_ref, n0_ref, m0_ref, bg_ref, gh_ref,
                  hm_ref, c_ref, n_ref, m_ref, *, valid):
    L = CHUNK
    nb = q_ref.shape[0]
    ci = pl.program_id(1)

    @pl.when(ci == 0)
    def _():
        c_ref[...] = c0_ref[...]
        n_ref[...] = n0_ref[...]
        m_ref[...] = m0_ref[...]

    row = lax.broadcasted_iota(jnp.int32, (L, L), 0)
    col = lax.broadcasted_iota(jnp.int32, (L, L), 1)
    tri = row >= col
    tril = jnp.where(tri, 1.0, 0.0).astype(BF16)
    lane = lax.broadcasted_iota(jnp.int32, (L, LANES), 1)
    head_lane = lax.broadcasted_iota(jnp.int32, (1, M_HEADS), 1)
    src = lax.broadcasted_iota(jnp.int32, (3 * LANES, 2 * LANES), 0) & (LANES - 1)
    dst = lax.broadcasted_iota(jnp.int32, (3 * LANES, 2 * LANES), 1)
    sel = [jnp.where(src == jnp.where(dst < LANES, M_HEADS + h, h), 1.0, 0.0).astype(BF16)
           for h in range(M_HEADS)]
    n_in = [n_ref[bi] for bi in range(nb)]
    m_in = [m_ref[bi] for bi in range(nb)]
    stores = []

    items = [(bi, h) for bi in range(nb) for h in range(M_HEADS)]
    gates = []
    for bi in range(nb):
        gf = gt_ref[bi] + bg_ref[...]
        lsg, _ = _log_sigmoid_parts(gf)
        li_all = gf
        if valid is not None:
            ok = (ci * L + lax.broadcasted_iota(jnp.int32, (L, LANES), 0)) < valid
            li_all = jnp.where(ok, gf, NEG_BIG)
            lsg = jnp.where(ok, lsg, 0.0)
        hi = lsg.astype(BF16)
        r1 = lsg - hi.astype(F32)
        mid = r1.astype(BF16)
        lo = (r1 - mid.astype(F32)).astype(BF16)
        bcum = _dot(tril, hi) + _dot(tril, mid) + _dot(tril, lo)
        gmat = jnp.where(lane < M_HEADS, li_all, bcum)
        gmat_t = gmat.T
        ghi = gmat.astype(BF16)
        g1 = gmat - ghi.astype(F32)
        gmid = g1.astype(BF16)
        g3 = jnp.concatenate([ghi, gmid, (g1 - gmid.astype(F32)).astype(BF16)], axis=1)
        gates.append((gmat_t, g3))

    sl = [slice(h * M_HEAD_DIM, (h + 1) * M_HEAD_DIM) for h in range(M_HEADS)]
    qs = [q_ref[bi, :, sl[h]] for bi, h in items]
    ks = [k_ref[bi, :, sl[h]] for bi, h in items]
    vs = [v_ref[bi, :, sl[h]] for bi, h in items]
    cs = [c_ref[bi, h] for bi, h in items]
    ns = [n_in[bi][h:h + 1, :] for bi, h in items]
    li_row = [gates[bi][0][h:h + 1, :] for bi, h in items]
    b_row = [gates[bi][0][M_HEADS + h:M_HEADS + h + 1, :] for bi, h in items]
    m_prev = [jnp.sum(jnp.where(head_lane == h, m_in[bi], 0.0), axis=-1, keepdims=True) for bi, h in items]
    idx = range(len(items))

    rep = [_dot(gates[bi][1], sel[h]) for bi, h in items]
    b_col = [r[:, :LANES] for r in rep]
    li_col = [r[:, LANES:] for r in rep]
    s_qk = [_dot_nt(qs[i], ks[i]) for i in idx]
    q_c = [_dot(qs[i], cs[i].astype(BF16)) for i in idx]
    q_n = []
    for i in idx:
        nhi, nlo = _split2(jnp.broadcast_to(ns[i], (M_HEAD_DIM, M_HEAD_DIM)))
        q_n.append(_dot_nt(qs[i], nhi) + _dot_nt(qs[i], nlo))

    dmat = [jnp.where(tri, b_col[i] - b_row[i] + li_row[i], NEG_BIG) for i in idx]
    row_max = [jnp.max(dmat[i], axis=-1, keepdims=True) for i in idx]
    m_inter = [b_col[i] + m_prev[i] for i in idx]
    m_t = [jnp.maximum(m_inter[i], row_max[i]) for i in idx]
    w_inter = [jnp.exp(m_inter[i] - m_t[i]) for i in idx]
    w_intra = [jnp.exp(dmat[i] - m_t[i]) * s_qk[i] for i in idx]
    w_sum = [jnp.sum(w_intra[i], axis=-1, keepdims=True) for i in idx]
    w_v = [_dot(w_intra[i].astype(BF16), vs[i]) for i in idx]
    hh = []
    for i in idx:
        num = w_inter[i] * q_c[i] + w_v[i]
        den = w_inter[i] * q_n[i] + w_sum[i]
        hh.append(num / jnp.maximum(jnp.abs(den), jnp.exp(-m_t[i])))
    h_ms = [jnp.mean(hh[i] * hh[i], axis=-1, keepdims=True) for i in idx]
    for i, (bi, h) in enumerate(items):
        hn = hh[i] * lax.rsqrt(h_ms[i] + RMS_EPS) * gh_ref[:, sl[h]]
        stores.append((hm_ref, (bi, slice(None), sl[h]),
                       (jax.nn.sigmoid(o_ref[bi, :, sl[h]]) * hn).astype(BF16)))

    b_last = [b_col[i][L - 1:L, :] for i in idx]
    g_max = [jnp.max(b_last[i] - b_row[i] + li_row[i], axis=-1, keepdims=True) for i in idx]
    m_new = [jnp.maximum(b_last[i] + m_prev[i], g_max[i]) for i in idx]
    kg = [ks[i].astype(F32) * jnp.exp(b_last[i] - b_col[i] + li_col[i] - m_new[i]) for i in idx]
    decay = [jnp.exp(b_last[i] + m_prev[i] - m_new[i]) for i in idx]
    kg_t = [kg[i].T.astype(BF16) for i in idx]
    for i, (bi, h) in enumerate(items):
        stores.append((c_ref, (bi, h), decay[i] * cs[i] + _dot(kg_t[i], vs[i])))
    for bi in range(nb):
        mine = [i for i in idx if items[i][0] == bi]
        n_rows = [decay[i] * ns[i] + jnp.sum(kg[i], axis=0, keepdims=True) for i in mine]
        m_out = m_in[bi]
        for i in mine:
            m_out = jnp.where(head_lane == items[i][1], m_new[i][:, :M_HEADS], m_out)
        stores.append((n_ref, (bi,), jnp.concatenate(n_rows, axis=0)))
        stores.append((m_ref, (bi,), m_out))
    for ref, where, val in stores:
        ref[where] = val


def _mlstm(qc, kc, vm, om, gt, c0, n0, m0, b_gate_row, g_head, valid, name):
    b, t, _ = qc.shape
    nc = t // CHUNK
    nb = MLSTM_SEQS_PER_STEP
    tok = lambda w: pl.BlockSpec((nb, CHUNK, w), lambda i, c: (i, c, 0))
    per_b = lambda shape: pl.BlockSpec((nb,) + shape, lambda i, c: (i,) + (0,) * len(shape))
    const = lambda shape: pl.BlockSpec(shape, lambda i, c: (0,) * len(shape))
    hd = M_HEAD_DIM
    return pl.pallas_call(
        functools.partial(_mlstm_kernel, valid=valid),
        grid=(b // nb, nc),
        in_specs=[tok(M_WIDTH), tok(M_WIDTH), tok(M_WIDTH), tok(M_WIDTH), tok(LANES),
                  per_b((M_HEADS, hd, hd)), per_b((M_HEADS, hd)), per_b((1, M_HEADS)),
                  const((1, LANES)), const((1, M_WIDTH))],
        out_specs=[tok(M_WIDTH), per_b((M_HEADS, hd, hd)), per_b((M_HEADS, hd)), per_b((1, M_HEADS))],
        out_shape=[jax.ShapeDtypeStruct((b, t, M_WIDTH), BF16),
                   jax.ShapeDtypeStruct((b, M_HEADS, hd, hd), F32),
                   jax.ShapeDtypeStruct((b, M_HEADS, hd), F32),
                   jax.ShapeDtypeStruct((b, 1, M_HEADS), F32)],
        compiler_params=_cparams(("arbitrary", "arbitrary")),
        name=name,
    )(qc, kc, vm, om, gt, c0, n0, m0, b_gate_row, g_head)


def _sbp_kernel(bias_ref, q_ref, k_ref, v_ref, o_ref, carry_ref, acc_ref):
    blk = CHUNK
    tq = q_ref.shape[0]
    sub = tq // blk
    qi = pl.program_id(2)
    row = lax.broadcasted_iota(jnp.int32, (blk, blk), 0)
    col = lax.broadcasted_iota(jnp.int32, (blk, blk), 1)
    uo = jnp.concatenate([jnp.where(row > col, 1.0, 0.0), jnp.ones((blk, blk), F32)], axis=1).astype(BF16)
    uo2 = jnp.concatenate([uo, uo], axis=0)
    first_head = col < SB_HEAD_DIM
    bias2 = bias_ref[...]
    carry_ref[...] = jnp.zeros_like(carry_ref)
    acc_ref[...] = jnp.zeros_like(acc_ref)

    def head_pair_rows(x):
        zero = jnp.zeros_like(x)
        return jnp.concatenate([jnp.where(first_head, x, zero), jnp.where(first_head, zero, x)], axis=0)

    def tile(j, r0, diag):
        n = tq - r0
        ks = pl.ds(pl.multiple_of(j * blk, blk), blk)
        z = _dot_nt(q_ref[r0:tq, :], head_pair_rows(k_ref[ks, :])) + bias2
        ls, lstay = _log_sigmoid_parts(z)
        if diag:
            q_pos = qi * tq + r0 + lax.broadcasted_iota(jnp.int32, (n, 2 * blk), 0)
            k_pos = j * blk + (lax.broadcasted_iota(jnp.int32, (n, 2 * blk), 1) & (blk - 1))
            valid = k_pos < q_pos
            lstay = jnp.where(valid, lstay, 0.0)
        hi, lo = _split2(lstay)
        ra = _dot(jnp.concatenate([hi[:, :blk], lo[:, :blk]], axis=1), uo2)
        rb = _dot(jnp.concatenate([hi[:, blk:], lo[:, blk:]], axis=1), uo2)
        later = jnp.concatenate([ra[:, :blk], rb[:, :blk]], axis=1) + carry_ref[r0:tq, :]
        a = jnp.exp(ls + later)
        if diag:
            a = jnp.where(valid, a, 0.0)
        acc_ref[r0:tq, :] += _dot(a.astype(BF16), head_pair_rows(v_ref[ks, :]))
        carry_ref[r0:tq, :] += jnp.concatenate([ra[:, blk:], rb[:, blk:]], axis=1)

    for m in reversed(range(sub)):
        tile(qi * sub + m, m * blk, True)

    unroll = 2 if sub % 2 == 0 else 1

    def body(t, c):
        for s in range(unroll):
            tile(qi * sub - 1 - (t * unroll + s), 0, False)
        return c

    lax.fori_loop(0, qi * sub // unroll, body, 0)
    o_ref[...] = acc_ref[...].astype(BF16)


def _sb_prompt(q, k, v, bias):
    b, t, w = q.shape
    tq = min(SBQ, t)
    pairs = w // LANES
    per_pair = LANES // SB_HEAD_DIM
    bias2 = jnp.repeat(bias.reshape(pairs, per_pair), CHUNK, axis=1).reshape(pairs, 1, per_pair * CHUNK)
    return pl.pallas_call(
        _sbp_kernel,
        grid=(b, pairs, t // tq),
        in_specs=[pl.BlockSpec((None, 1, per_pair * CHUNK), lambda i, h, s: (h, 0, 0)),
                  pl.BlockSpec((None, tq, LANES), lambda i, h, s: (i, s, h)),
                  pl.BlockSpec((None, t, LANES), lambda i, h, s: (i, 0, h)),
                  pl.BlockSpec((None, t, LANES), lambda i, h, s: (i, 0, h))],
        out_specs=pl.BlockSpec((None, tq, LANES), lambda i, h, s: (i, s, h)),
        out_shape=jax.ShapeDtypeStruct((b, t, w), BF16),
        scratch_shapes=[pltpu.VMEM((tq, per_pair * CHUNK), F32), pltpu.VMEM((tq, LANES), F32)],
        compiler_params=_cparams(("arbitrary", "arbitrary", "arbitrary")),
        name="sb_prompt",
    )(bias2, q, k, v)


def _sbs_group(first, bias_ref, qb_ref, k_refs, v_refs, carry_ref, z_ref, a_ref, acc_ref):
    G = len(k_refs)
    H = SB_HEADS
    page = k_refs[0].shape[-1]
    for g in range(G):
        for h in range(H):
            z_ref[g * H + h:g * H + h + 1, :] = jnp.sum(k_refs[g][h] * qb_ref[h], axis=0, keepdims=True)
    ls, lstay = _log_sigmoid_parts(z_ref[...] + bias_ref[...])

    row = lax.broadcasted_iota(jnp.int32, (page, page), 0)
    col = lax.broadcasted_iota(jnp.int32, (page, page), 1)
    uo = jnp.concatenate([jnp.where(row > col, 1.0, 0.0), jnp.ones((page, page), F32)], axis=1).astype(BF16)
    hi, lo = _split2(lstay)
    r = _dot(jnp.concatenate([hi, lo], axis=1), jnp.concatenate([uo, uo], axis=0))
    within, tot = r[:, :page], r[:, page:]
    n = G * H
    rr = lax.broadcasted_iota(jnp.int32, (n, n), 0)
    rc = lax.broadcasted_iota(jnp.int32, (n, n), 1)
    shift = H.bit_length() - 1
    later_page = ((rr & (H - 1)) == (rc & (H - 1))) & ((rc >> shift) > (rr >> shift))
    mx = jnp.where(later_page, 1.0, 0.0).astype(BF16)
    thi, tlo = _split2(tot)
    carry = jnp.where(first, 0.0, carry_ref[...])
    a_ref[...] = jnp.exp(ls + within + _dot(mx, thi) + _dot(mx, tlo) + jnp.tile(carry, (G, 1)))
    for g in range(G):
        carry = carry + tot[g * H:(g + 1) * H, :]
    carry_ref[...] = carry

    for h in range(H):
        acc = jnp.where(first, 0.0, acc_ref[h])
        for g in range(G):
            acc = acc + v_refs[g][h] * a_ref[g * H + h:g * H + h + 1, :]
        acc_ref[h] = acc


def _sbs_finish(acc_ref, o_ref):
    page = acc_ref.shape[-1]
    ones = jnp.ones((SUBLANES, page), BF16)
    for h in range(SB_HEADS):
        x = acc_ref[h]
        xh = x.astype(BF16)
        r1 = x - xh.astype(F32)
        xm = r1.astype(BF16)
        xl = (r1 - xm.astype(F32)).astype(BF16)
        s = _dot_nt(ones, xh) + _dot_nt(ones, xm) + _dot_nt(ones, xl)
        o_ref[h:h + 1, :] = s[0:1, :]


def _sbs_kernel(pt_ref, bias_ref, qb_ref, *refs):
    del pt_ref
    G = PAGES_PER_STEP
    k_refs, v_refs = refs[:G], refs[G:2 * G]
    o_ref, carry_ref, z_ref, a_ref, acc_ref = refs[2 * G:]
    j = pl.program_id(1)
    _sbs_group(j == 0, bias_ref, qb_ref, k_refs, v_refs, carry_ref, z_ref, a_ref, acc_ref)

    @pl.when(j == pl.num_programs(1) - 1)
    def _():
        _sbs_finish(acc_ref, o_ref)


def _sb_sample(layer, q_bf, cache_k, cache_v, page_table, bias):
    b = q_bf.shape[0]
    n_pages = page_table.shape[1]
    page = cache_k.shape[2]
    G = PAGES_PER_STEP
    H = SB_HEADS
    groups = n_pages // G
    kt = jnp.transpose(cache_k, (0, 1, 3, 4, 2))
    vt = jnp.transpose(cache_v, (0, 1, 3, 4, 2))
    qb = jnp.broadcast_to(q_bf.astype(F32).reshape(b, H, SB_HEAD_DIM, 1), (b, H, SB_HEAD_DIM, page))
    bias_rows = jnp.broadcast_to(jnp.tile(bias, G).reshape(G * H, 1), (G * H, page))

    def page_spec(g):
        return pl.BlockSpec(
            (None, None, H, SB_HEAD_DIM, page),
            lambda i, j, pt: (layer, pt[i, (groups - 1 - j) * G + g], 0, 0, 0))

    out = pl.pallas_call(
        _sbs_kernel,
        grid_spec=pltpu.PrefetchScalarGridSpec(
            num_scalar_prefetch=1,
            grid=(b, groups),
            in_specs=[pl.BlockSpec((G * H, page), lambda i, j, pt: (0, 0)),
                      pl.BlockSpec((None, H, SB_HEAD_DIM, page), lambda i, j, pt: (i, 0, 0, 0))]
                     + [page_spec(g) for g in range(G)] * 2,
            out_specs=pl.BlockSpec((None, H, SB_HEAD_DIM), lambda i, j, pt: (i, 0, 0)),
            scratch_shapes=[pltpu.VMEM((H, page), F32), pltpu.VMEM((G * H, page), F32),
                            pltpu.VMEM((G * H, page), F32), pltpu.VMEM((H, SB_HEAD_DIM, page), F32)],
        ),
        out_shape=jax.ShapeDtypeStruct((b, H, SB_HEAD_DIM), F32),
        compiler_params=_cparams(("arbitrary", "arbitrary")),
        name="sb_sample",
    )(page_table, bias_rows, qb, *([kt] * G), *([vt] * G))
    return out.reshape(b, SB_WIDTH).astype(BF16)


def _ffn_kernel(x_ref, hm_ref, hs_ref, ga1_ref, sh2_ref, sc2_ref, ga2_ref, g2_ref,
                wo_ref, wg_ref, wu_ref, wd_ref, *rest, final):
    if final:
        gf_ref, shf_ref, scf_ref, o_ref = rest
    else:
        o_ref, = rest
    attn = _dot(hm_ref[...], wo_ref[0:M_WIDTH, :]) + _dot(hs_ref[...], wo_ref[M_WIDTH:, :])
    x1 = x_ref[...] + ga1_ref[...] * attn
    h2 = _rms_mod(x1, g2_ref[...], sc2_ref[...], sh2_ref[...]).astype(BF16)
    gate = _dot(h2, wg_ref[...])
    up = _dot(h2, wu_ref[...])
    act = (gate * jax.nn.sigmoid(gate) * up).astype(BF16)
    x2 = x1 + ga2_ref[...] * _dot(act, wd_ref[...])
    if final:
        x2 = _rms_mod(x2, gf_ref[...], scf_ref[...], shf_ref[...])
    o_ref[...] = x2


def _ffn(x2, hm, hs, mod, g2, wo, wg, wu, wd, group, tm, seq, n_sample, final=None):
    n, d = x2.shape
    mods = [_mod_operand(mod, c, group, tm, seq, n_sample, d) for c in (2, 3, 4, 5)]
    row = lambda w: pl.BlockSpec((tm, w), lambda i: (i, 0))
    vec = pl.BlockSpec((1, d), lambda i: (0, 0))
    resident = lambda w: pl.BlockSpec(w.shape, lambda i: (0, 0), pipeline_mode=pl.Buffered(1))
    operands = [x2, hm, hs] + [a for a, _ in mods] + [g2, wo, wg, wu, wd]
    in_specs = ([row(d), row(M_WIDTH), row(SB_WIDTH)] + [s for _, s in mods]
                + [vec, resident(wo), resident(wg), resident(wu), resident(wd)])
    if final is not None:
        mod_f, g_f = final
        fmods = [_mod_operand(mod_f, c, group, tm, seq, n_sample, d) for c in (0, 1)]
        operands += [g_f] + [a for a, _ in fmods]
        in_specs += [vec] + [s for _, s in fmods]
    return pl.pallas_call(
        functools.partial(_ffn_kernel, final=final is not None),
        grid=(n // tm,),
        in_specs=in_specs,
        out_specs=row(d),
        out_shape=jax.ShapeDtypeStruct((n, d), F32),
        compiler_params=_cparams(("arbitrary",)),
        name="ffn_" + group,
    )(*operands)


def kernel(x_prompt, x_sample, cache_sb_k, cache_sb_v, state_conv, state_mlstm_C, state_mlstm_n,
           state_mlstm_m, page_table, c_prompt, c_sample, w_ada, b_ada, norm1_g, w_in, b_gate, sb_bias,
           conv_w, conv_b, head_norm_g, w_out, norm2_g, w_gate, w_up, w_down, w_ada_f, b_ada_f, norm_f_g):
    bp, seq, d = x_prompt.shape
    bs = x_sample.shape[0]
    depth = w_ada.shape[0]
    n_gate = 2 * M_HEADS
    assert x_sample.shape[1] == 1 and seq % CHUNK == 0
    assert page_table.shape[1] % PAGES_PER_STEP == 0

    c_all = jnp.concatenate([c_sample, c_prompt], axis=0)
    xp = x_prompt.reshape(bp * seq, d)
    xs = x_sample.reshape(bs, d)
    tm_p = min(1024, seq)
    tm_f = min(512, seq)
    assert seq % tm_p == 0 and seq % tm_f == 0

    assert bp % MLSTM_SEQS_PER_STEP == 0 and bs % MLSTM_SEQS_PER_STEP == 0
    zeros_c = jnp.zeros((bp, M_HEADS, M_HEAD_DIM, M_HEAD_DIM), F32)
    zeros_n = jnp.zeros((bp, M_HEADS, M_HEAD_DIM), F32)
    zeros_m = jnp.zeros((bp, 1, M_HEADS), F32)

    outs = {k: [] for k in ("kp", "vp", "ks", "vs", "cvp", "cvs", "Cp", "Cs", "np", "ns", "mp", "ms")}
    mod_f = _ada(c_all, w_ada_f[None], b_ada_f[None], 0)
    gf = norm_f_g.reshape(1, d)
    for l in range(depth):
        mod = _ada(c_all, w_ada, b_ada, l)
        wt = _pack_w_in(w_in[l])
        wo, wg, wu, wd = (w_out[l].astype(BF16), w_gate[l].astype(BF16), w_up[l].astype(BF16),
                          w_down[l].astype(BF16))
        g1 = norm1_g[l].reshape(1, d)
        g2 = norm2_g[l].reshape(1, d)
        bg_row = jnp.pad(b_gate[l], (0, LANES - n_gate)).reshape(1, LANES)
        cb = conv_b[l].reshape(1, 2 * M_WIDTH)
        gh = head_norm_g[l].reshape(1, M_WIDTH)
        final = (mod_f, gf) if l == depth - 1 else None

        qc, kc, vm, om, gt, q, kb, vb, kt, vt, cs = _inproj_prompt(
            xp, mod, g1, wt, conv_w[l], cb, tm_p, seq, bs)
        r3 = lambda a: a.reshape(bp, seq, a.shape[-1])
        hm, c_new, n_new, m_new = _mlstm(r3(qc), r3(kc), r3(vm), r3(om), r3(gt), zeros_c, zeros_n,
                                         zeros_m, bg_row, gh, None, "mlstm_prompt")
        hs = _sb_prompt(r3(q), r3(kb), r3(vb), sb_bias[l])
        xp = _ffn(xp, hm.reshape(bp * seq, M_WIDTH), hs.reshape(bp * seq, SB_WIDTH), mod, g2,
                  wo, wg, wu, wd, "prompt", tm_f, seq, bs, final)
        outs["kp"].append(kt)
        outs["vp"].append(vt)
        outs["cvp"].append(cs[:, SUBLANES - (CONV_W - 1):, :])
        outs["Cp"].append(c_new)
        outs["np"].append(n_new)
        outs["mp"].append(m_new.reshape(bp, M_HEADS))

        qk, qc, kc, vm, om, gt, q, kf, vf = _inproj_sample(xs, mod, g1, wt, conv_w[l], cb, state_conv[l])
        pad_t = lambda a: jnp.pad(a[:, None, :], ((0, 0), (0, CHUNK - 1), (0, 0)))
        hm, c_new, n_new, m_new = _mlstm(pad_t(qc), pad_t(kc), pad_t(vm), pad_t(om), pad_t(gt),
                                         state_mlstm_C[l], state_mlstm_n[l],
                                         state_mlstm_m[l].reshape(bs, 1, M_HEADS),
                                         bg_row, gh, 1, "mlstm_sample")
        hs = _sb_sample(l, q, cache_sb_k, cache_sb_v, page_table, sb_bias[l])
        xs = _ffn(xs, hm[:, 0, :], hs, mod, g2, wo, wg, wu, wd, "sample", bs, 1, bs, final)
        outs["ks"].append(kf.reshape(bs, 1, SB_HEADS, SB_HEAD_DIM))
        outs["vs"].append(vf.reshape(bs, 1, SB_HEADS, SB_HEAD_DIM))
        outs["cvs"].append(jnp.concatenate([state_conv[l][:, 1:, :], qk[:, None, :]], axis=1))
        outs["Cs"].append(c_new)
        outs["ns"].append(n_new)
        outs["ms"].append(m_new.reshape(bs, M_HEADS))

    y_prompt = xp.reshape(bp, seq, d)
    y_sample = xs.reshape(bs, 1, d)
    st = lambda k: jnp.stack(outs[k])
    tok_major = lambda a: jnp.transpose(a.reshape(depth, bp, SB_HEADS, SB_HEAD_DIM, seq), (0, 1, 4, 2, 3))
    return (y_prompt, y_sample, tok_major(st("kp")), tok_major(st("vp")), st("ks"), st("vs"),
            st("cvp"), st("cvs"), st("Cp"), st("Cs"), st("np"), st("ns"), st("mp"), st("ms"))
```

```python
import functools

import jax
import jax.numpy as jnp
from jax import lax
from jax.experimental import pallas as pl
from jax.experimental.pallas import tpu as pltpu

F32 = jnp.float32
BF16 = jnp.bfloat16

RMS_EPS = 1e-6
M_HEADS = 4
M_HEAD_DIM = 128
M_WIDTH = M_HEADS * M_HEAD_DIM
SB_HEADS = 8
SB_HEAD_DIM = 64
SB_WIDTH = SB_HEADS * SB_HEAD_DIM
CONV_W = 4
CHUNK = 128
LANES = 128
SUBLANES = 8
PAGES_PER_STEP = 16
SBQ = 512
MLSTM_SEQS_PER_STEP = 8
LOG2_E = 1.4426950408889634
NEG_BIG = -1e30
VMEM_LIMIT = 56 * 1024 * 1024


def _cparams(sem):
    return pltpu.CompilerParams(dimension_semantics=sem, vmem_limit_bytes=VMEM_LIMIT)


def _log_sigmoid_parts(z):
    l1p = jnp.log(1.0 + jnp.exp2(jnp.abs(z) * (-LOG2_E)))
    ls = jnp.minimum(z, 0.0) - l1p
    return ls, ls - z


def _split2(x):
    hi = x.astype(BF16)
    lo = (x - hi.astype(F32)).astype(BF16)
    return hi, lo


def _rms_mod(x, g, sc, sh):
    y = x * lax.rsqrt(jnp.mean(x * x, axis=-1, keepdims=True) + RMS_EPS)
    return (y * g) * (1.0 + sc) + sh


def _dot(a, b):
    return jnp.dot(a, b, preferred_element_type=F32)


def _dot_nt(a, b):
    return lax.dot_general(a, b, (((1,), (1,)), ((), ())), preferred_element_type=F32)


def _ada_kernel(c_ref, w_ref, b_ref, o_ref):
    c = c_ref[...]
    a = (c * jax.nn.sigmoid(c)).astype(BF16)
    o_ref[...] = _dot(a, w_ref[...].astype(BF16)) + b_ref[...]


def _ada(c_all, w, b, layer):
    m, d = c_all.shape
    n = w.shape[2]
    tn = 1024
    return pl.pallas_call(
        _ada_kernel,
        grid=(n // tn,),
        in_specs=[pl.BlockSpec((m, d), lambda j: (0, 0)),
                  pl.BlockSpec((None, d, tn), lambda j: (layer, 0, j)),
                  pl.BlockSpec((None, 1, tn), lambda j: (layer, 0, j))],
        out_specs=pl.BlockSpec((m, tn), lambda j: (0, j)),
        out_shape=jax.ShapeDtypeStruct((m, n), F32),
        compiler_params=_cparams(("arbitrary",)),
        name="ada_mod",
    )(c_all, w, b.reshape(b.shape[0], 1, n))


def _mod_operand(mod, chunk, group, tm, seq, n_sample, d):
    if group == "sample":
        return mod, pl.BlockSpec((n_sample, d), lambda i, *_: (0, chunk))
    mod3 = mod.reshape(mod.shape[0], 1, mod.shape[1])
    return mod3, pl.BlockSpec((None, 1, d), lambda i, *_: (n_sample + (i * tm) // seq, 0, chunk))


_SEG_QK = (0, 2 * M_WIDTH)
_SEG_VM = (_SEG_QK[1], _SEG_QK[1] + M_WIDTH)
_SEG_OM = (_SEG_VM[1], _SEG_VM[1] + M_WIDTH)
_SEG_Q = (_SEG_OM[1], _SEG_OM[1] + SB_WIDTH)
_SEG_K = (_SEG_Q[1], _SEG_Q[1] + SB_WIDTH)
_SEG_V = (_SEG_K[1], _SEG_K[1] + SB_WIDTH)
_SEG_GATES = (_SEG_V[1], _SEG_V[1] + LANES)


def _pack_w_in(w_in_l):
    n_gate = 2 * M_HEADS
    g0 = 4 * M_WIDTH
    wt = w_in_l.T
    return jnp.concatenate(
        [wt[:g0], wt[g0 + n_gate:], jnp.pad(wt[g0:g0 + n_gate], ((0, LANES - n_gate), (0, 0)))],
        axis=0).astype(BF16)


def _conv_silu_split(acc, qc_ref, kc_ref):
    qkc = acc * jax.nn.sigmoid(acc)
    qc_ref[...] = qkc[:, :M_WIDTH]
    kc_ref[...] = qkc[:, M_WIDTH:] * (M_HEAD_DIM ** -0.5)


def _inproj_prompt_kernel(x_ref, g_ref, sc_ref, sh_ref, wt_ref, cw_ref, cb_ref,
                          qc_ref, kc_ref, vm_ref, om_ref, gt_ref, q_ref, kb_ref, vb_ref,
                          kt_ref, vt_ref, cs_ref, ubuf, *, tiles_per_seq):
    tm = x_ref.shape[0]
    h = _rms_mod(x_ref[...], g_ref[...], sc_ref[...], sh_ref[...]).astype(BF16)
    seg = lambda s: _dot_nt(h, wt_ref[s[0]:s[1], :])

    @pl.when(pl.program_id(0) % tiles_per_seq == 0)
    def _():
        ubuf[0:SUBLANES, :] = jnp.zeros((SUBLANES, ubuf.shape[1]), F32)

    def conv_silu(c0, c1, dst_ref, d0, scale):
        cols = slice(c0, c1)
        u = _dot_nt(h, wt_ref[c0:c1, :])
        ubuf[SUBLANES:SUBLANES + tm, cols] = u
        acc = cb_ref[:, cols] + ubuf[5:5 + tm, cols] * cw_ref[0:1, cols]
        acc = acc + ubuf[6:6 + tm, cols] * cw_ref[1:2, cols]
        acc = acc + ubuf[7:7 + tm, cols] * cw_ref[2:3, cols]
        acc = acc + u * cw_ref[3:4, cols]
        last = ubuf[tm:tm + SUBLANES, cols]
        ubuf[0:SUBLANES, cols] = last
        cs_ref[:, cols] = last
        y = acc * jax.nn.sigmoid(acc)
        if scale is not None:
            y = y * scale
        dst_ref[:, d0:d0 + (c1 - c0)] = y.astype(BF16)

    half = M_WIDTH // 2
    conv_silu(0, half, qc_ref, 0, None)
    vm_ref[...] = seg(_SEG_VM).astype(BF16)
    conv_silu(half, M_WIDTH, qc_ref, half, None)
    om_ref[...] = seg(_SEG_OM)
    conv_silu(M_WIDTH, M_WIDTH + half, kc_ref, 0, M_HEAD_DIM ** -0.5)
    q_ref[...] = (seg(_SEG_Q) * (SB_HEAD_DIM ** -0.5)).astype(BF16)
    conv_silu(M_WIDTH + half, 2 * M_WIDTH, kc_ref, half, M_HEAD_DIM ** -0.5)
    gt_ref[...] = seg(_SEG_GATES)
    k = seg(_SEG_K)
    kb_ref[...] = k.astype(BF16)
    kt_ref[...] = k.T
    v = seg(_SEG_V)
    vb_ref[...] = v.astype(BF16)
    vt_ref[...] = v.T


def _inproj_prompt(x2, mod, g1, wt, conv_w, conv_b, tm, seq, n_sample):
    n, d = x2.shape
    b = n // seq
    tps = seq // tm
    sh_a, sh_s = _mod_operand(mod, 0, "prompt", tm, seq, n_sample, d)
    sc_a, sc_s = _mod_operand(mod, 1, "prompt", tm, seq, n_sample, d)
    row = lambda w: pl.BlockSpec((tm, w), lambda i: (i, 0))
    const = lambda a: pl.BlockSpec(a.shape, lambda i: (0,) * a.ndim)
    tr = pl.BlockSpec((None, SB_WIDTH, tm), lambda i: (i // tps, 0, i % tps))
    rows = [(M_WIDTH, BF16), (M_WIDTH, BF16), (M_WIDTH, BF16), (M_WIDTH, F32), (LANES, F32),
            (SB_WIDTH, BF16), (SB_WIDTH, BF16), (SB_WIDTH, BF16)]
    return pl.pallas_call(
        functools.partial(_inproj_prompt_kernel, tiles_per_seq=tps),
        grid=(n // tm,),
        in_specs=[row(d), const(g1), sc_s, sh_s,
                  pl.BlockSpec(wt.shape, lambda i: (0, 0), pipeline_mode=pl.Buffered(1)),
                  const(conv_w), const(conv_b)],
        out_specs=[row(w) for w, _ in rows] + [tr, tr,
                   pl.BlockSpec((None, SUBLANES, 2 * M_WIDTH), lambda i: (i // tps, 0, 0))],
        out_shape=[jax.ShapeDtypeStruct((n, w), dt) for w, dt in rows]
                  + [jax.ShapeDtypeStruct((b, SB_WIDTH, seq), F32)] * 2
                  + [jax.ShapeDtypeStruct((b, SUBLANES, 2 * M_WIDTH), F32)],
        scratch_shapes=[pltpu.VMEM((SUBLANES + tm, 2 * M_WIDTH), F32)],
        compiler_params=_cparams(("arbitrary",)),
        name="in_proj_prompt",
    )(x2, g1, sc_a, sh_a, wt, conv_w, conv_b)


def _inproj_sample_kernel(x_ref, g_ref, sc_ref, sh_ref, wt_ref, cw_ref, cb_ref, s0_ref, s1_ref, s2_ref,
                          qk_ref, qc_ref, kc_ref, vm_ref, om_ref, gt_ref, q_ref, kf_ref, vf_ref):
    h = _rms_mod(x_ref[...], g_ref[...], sc_ref[...], sh_ref[...]).astype(BF16)
    seg = lambda s: _dot_nt(h, wt_ref[s[0]:s[1], :])
    u = seg(_SEG_QK)
    qk_ref[...] = u
    cw = cw_ref[...]
    acc = cb_ref[...] + s0_ref[...] * cw[0:1, :]
    acc = acc + s1_ref[...] * cw[1:2, :]
    acc = acc + s2_ref[...] * cw[2:3, :]
    acc = acc + u * cw[3:4, :]
    _conv_silu_split(acc, qc_ref, kc_ref)
    vm_ref[...] = seg(_SEG_VM)
    om_ref[...] = seg(_SEG_OM)
    gt_ref[...] = seg(_SEG_GATES)
    q_ref[...] = (seg(_SEG_Q) * (SB_HEAD_DIM ** -0.5)).astype(BF16)
    kf_ref[...] = seg(_SEG_K)
    vf_ref[...] = seg(_SEG_V)


def _inproj_sample(x2, mod, g1, wt, conv_w, conv_b, conv_state):
    n, d = x2.shape
    sh_a, sh_s = _mod_operand(mod, 0, "sample", n, 1, n, d)
    sc_a, sc_s = _mod_operand(mod, 1, "sample", n, 1, n, d)
    full = lambda a: pl.BlockSpec(a.shape, lambda i: (0,) * a.ndim)
    states = [conv_state[:, r, :] for r in range(CONV_W - 1)]
    outs = [(2 * M_WIDTH, F32), (M_WIDTH, F32), (M_WIDTH, F32), (M_WIDTH, F32), (M_WIDTH, F32),
            (LANES, F32), (SB_WIDTH, BF16), (SB_WIDTH, F32), (SB_WIDTH, F32)]
    return pl.pallas_call(
        _inproj_sample_kernel,
        grid=(1,),
        in_specs=[full(x2), full(g1), sc_s, sh_s, full(wt), full(conv_w), full(conv_b)]
                 + [full(s) for s in states],
        out_specs=[pl.BlockSpec((n, w), lambda i: (0, 0)) for w, _ in outs],
        out_shape=[jax.ShapeDtypeStruct((n, w), dt) for w, dt in outs],
        compiler_params=_cparams(("arbitrary",)),
        name="in_proj_sample",
    )(x2, g1, sc_a, sh_a, wt, conv_w, conv_b, *states)


def _mlstm_kernel(q_ref, k_ref, v_ref, o_ref, gt_ref, c0_ref, n0_ref, m0_ref, bg_ref, gh_ref,
                  hm_ref, c_ref, n_ref, m_ref):
    L = CHUNK
    nb = q_ref.shape[0]
    ci = pl.program_id(1)

    @pl.when(ci == 0)
    def _():
        c_ref[...] = c0_ref[...]
        n_ref[...] = n0_ref[...]
        m_ref[...] = m0_ref[...]

    row = lax.broadcasted_iota(jnp.int32, (L, L), 0)
    col = lax.broadcasted_iota(jnp.int32, (L, L), 1)
    tri = row >= col
    tril = jnp.where(tri, 1.0, 0.0).astype(BF16)
    lane = lax.broadcasted_iota(jnp.int32, (L, LANES), 1)
    head_lane = lax.broadcasted_iota(jnp.int32, (1, M_HEADS), 1)
    src = lax.broadcasted_iota(jnp.int32, (3 * LANES, 2 * LANES), 0) & (LANES - 1)
    dst = lax.broadcasted_iota(jnp.int32, (3 * LANES, 2 * LANES), 1)
    sel = [jnp.where(src == jnp.where(dst < LANES, M_HEADS + h, h), 1.0, 0.0).astype(BF16)
           for h in range(M_HEADS)]
    n_in = [n_ref[bi] for bi in range(nb)]
    m_in = [m_ref[bi] for bi in range(nb)]
    stores = []

    items = [(bi, h) for bi in range(nb) for h in range(M_HEADS)]
    gates = []
    for bi in range(nb):
        gf = gt_ref[bi] + bg_ref[...]
        lsg, _ = _log_sigmoid_parts(gf)
        li_all = gf
        hi = lsg.astype(BF16)
        r1 = lsg - hi.astype(F32)
        mid = r1.astype(BF16)
        lo = (r1 - mid.astype(F32)).astype(BF16)
        bcum = _dot(tril, hi) + _dot(tril, mid) + _dot(tril, lo)
        gmat = jnp.where(lane < M_HEADS, li_all, bcum)
        gmat_t = gmat.T
        ghi = gmat.astype(BF16)
        g1 = gmat - ghi.astype(F32)
        gmid = g1.astype(BF16)
        g3 = jnp.concatenate([ghi, gmid, (g1 - gmid.astype(F32)).astype(BF16)], axis=1)
        gates.append((gmat_t, g3))

    sl = [slice(h * M_HEAD_DIM, (h + 1) * M_HEAD_DIM) for h in range(M_HEADS)]
    qs = [q_ref[bi, :, sl[h]] for bi, h in items]
    ks = [k_ref[bi, :, sl[h]] for bi, h in items]
    vs = [v_ref[bi, :, sl[h]] for bi, h in items]
    cs = [c_ref[bi, h] for bi, h in items]
    ns = [n_in[bi][h:h + 1, :] for bi, h in items]
    li_row = [gates[bi][0][h:h + 1, :] for bi, h in items]
    b_row = [gates[bi][0][M_HEADS + h:M_HEADS + h + 1, :] for bi, h in items]
    m_prev = [jnp.sum(jnp.where(head_lane == h, m_in[bi], 0.0), axis=-1, keepdims=True) for bi, h in items]
    idx = range(len(items))

    rep = [_dot(gates[bi][1], sel[h]) for bi, h in items]
    b_col = [r[:, :LANES] for r in rep]
    li_col = [r[:, LANES:] for r in rep]
    s_qk = [_dot_nt(qs[i], ks[i]) for i in idx]
    q_c = [_dot(qs[i], cs[i].astype(BF16)) for i in idx]
    q_n = []
    for i in idx:
        nhi, nlo = _split2(jnp.broadcast_to(ns[i], (M_HEAD_DIM, M_HEAD_DIM)))
        q_n.append(_dot_nt(qs[i], nhi) + _dot_nt(qs[i], nlo))

    dmat = [jnp.where(tri, b_col[i] - b_row[i] + li_row[i], NEG_BIG) for i in idx]
    row_max = [jnp.max(dmat[i], axis=-1, keepdims=True) for i in idx]
    m_inter = [b_col[i] + m_prev[i] for i in idx]
    m_t = [jnp.maximum(m_inter[i], row_max[i]) for i in idx]
    w_inter = [jnp.exp(m_inter[i] - m_t[i]) for i in idx]
    w_intra = [jnp.exp(dmat[i] - m_t[i]) * s_qk[i] for i in idx]
    w_sum = [jnp.sum(w_intra[i], axis=-1, keepdims=True) for i in idx]
    w_v = [_dot(w_intra[i].astype(BF16), vs[i]) for i in idx]
    hh = []
    for i in idx:
        num = w_inter[i] * q_c[i] + w_v[i]
        den = w_inter[i] * q_n[i] + w_sum[i]
        hh.append(num / jnp.maximum(jnp.abs(den), jnp.exp(-m_t[i])))
    h_ms = [jnp.mean(hh[i] * hh[i], axis=-1, keepdims=True) for i in idx]
    for i, (bi, h) in enumerate(items):
        hn = hh[i] * lax.rsqrt(h_ms[i] + RMS_EPS) * gh_ref[:, sl[h]]
        stores.append((hm_ref, (bi, slice(None), sl[h]),
                       (jax.nn.sigmoid(o_ref[bi, :, sl[h]]) * hn).astype(BF16)))

    b_last = [b_col[i][L - 1:L, :] for i in idx]
    g_max = [jnp.max(b_last[i] - b_row[i] + li_row[i], axis=-1, keepdims=True) for i in idx]
    m_new = [jnp.maximum(b_last[i] + m_prev[i], g_max[i]) for i in idx]
    kg = [ks[i].astype(F32) * jnp.exp(b_last[i] - b_col[i] + li_col[i] - m_new[i]) for i in idx]
    decay = [jnp.exp(b_last[i] + m_prev[i] - m_new[i]) for i in idx]
    kg_t = [kg[i].T.astype(BF16) for i in idx]
    for i, (bi, h) in enumerate(items):
        stores.append((c_ref, (bi, h), decay[i] * cs[i] + _dot(kg_t[i], vs[i])))
    for bi in range(nb):
        mine = [i for i in idx if items[i][0] == bi]
        n_rows = [decay[i] * ns[i] + jnp.sum(kg[i], axis=0, keepdims=True) for i in mine]
        m_out = m_in[bi]
        for i in mine:
            m_out = jnp.where(head_lane == items[i][1], m_new[i][:, :M_HEADS], m_out)
        stores.append((n_ref, (bi,), jnp.concatenate(n_rows, axis=0)))
        stores.append((m_ref, (bi,), m_out))
    for ref, where, val in stores:
        ref[where] = val


def _mlstm(qc, kc, vm, om, gt, c0, n0, m0, b_gate_row, g_head):
    b, t, _ = qc.shape
    nc = t // CHUNK
    nb = MLSTM_SEQS_PER_STEP
    tok = lambda w: pl.BlockSpec((nb, CHUNK, w), lambda i, c: (i, c, 0))
    per_b = lambda shape: pl.BlockSpec((nb,) + shape, lambda i, c: (i,) + (0,) * len(shape))
    const = lambda shape: pl.BlockSpec(shape, lambda i, c: (0,) * len(shape))
    hd = M_HEAD_DIM
    return pl.pallas_call(
        _mlstm_kernel,
        grid=(b // nb, nc),
        in_specs=[tok(M_WIDTH), tok(M_WIDTH), tok(M_WIDTH), tok(M_WIDTH), tok(LANES),
                  per_b((M_HEADS, hd, hd)), per_b((M_HEADS, hd)), per_b((1, M_HEADS)),
                  const((1, LANES)), const((1, M_WIDTH))],
        out_specs=[tok(M_WIDTH), per_b((M_HEADS, hd, hd)), per_b((M_HEADS, hd)), per_b((1, M_HEADS))],
        out_shape=[jax.ShapeDtypeStruct((b, t, M_WIDTH), BF16),
                   jax.ShapeDtypeStruct((b, M_HEADS, hd, hd), F32),
                   jax.ShapeDtypeStruct((b, M_HEADS, hd), F32),
                   jax.ShapeDtypeStruct((b, 1, M_HEADS), F32)],
        compiler_params=_cparams(("arbitrary", "arbitrary")),
        name="mlstm_prompt",
    )(qc, kc, vm, om, gt, c0, n0, m0, b_gate_row, g_head)


def _mlstm_step_kernel(q_ref, k_ref, v_ref, o_ref, gt_ref, c0_ref, n0_ref, m0_ref, bg_ref, gh_ref,
                       hm_ref, c_ref, n_ref, m_ref):
    nb = q_ref.shape[0]
    hd = M_HEAD_DIM
    gf = gt_ref[...] + bg_ref[...]
    lsg, _ = _log_sigmoid_parts(gf)
    lf = pltpu.roll(lsg, LANES - M_HEADS, axis=1)
    m_old = m0_ref[...]
    m_new = jnp.maximum(lf + m_old, gf)
    decay_all = jnp.exp(lf + m_old - m_new)
    gain_all = jnp.exp(gf - m_new)
    floor_all = jnp.exp(-m_new)
    m_ref[...] = m_new

    lane = lax.broadcasted_iota(jnp.int32, (1, LANES), 1)

    def pick(x, bi, h):
        return jnp.sum(jnp.where(lane == h, x[bi:bi + 1, :], 0.0), axis=-1, keepdims=True)

    items = [(bi, h) for bi in range(nb) for h in range(M_HEADS)]
    sl = [slice(h * hd, (h + 1) * hd) for h in range(M_HEADS)]
    idx = range(len(items))
    decay = [pick(decay_all, bi, h) for bi, h in items]
    gain = [pick(gain_all, bi, h) for bi, h in items]
    floor = [pick(floor_all, bi, h) for bi, h in items]
    qs = [q_ref[bi:bi + 1, sl[h]] for bi, h in items]
    vs = [v_ref[bi:bi + 1, sl[h]] for bi, h in items]
    kg = [k_ref[bi:bi + 1, sl[h]] * gain[i] for i, (bi, h) in enumerate(items)]
    kg_col = [jnp.broadcast_to(kg[i], (hd, hd)).T for i in idx]
    c_new = [decay[i] * c0_ref[bi, h] + kg_col[i] * vs[i] for i, (bi, h) in enumerate(items)]
    n_new = [decay[i] * n0_ref[bi, h:h + 1, :] + kg[i] for i, (bi, h) in enumerate(items)]
    num = [_dot(jnp.broadcast_to(qs[i], (SUBLANES, hd)).astype(BF16), c_new[i].astype(BF16))[0:1, :]
           for i in idx]
    den = [jnp.sum(qs[i] * n_new[i], axis=-1, keepdims=True) for i in idx]
    hh = [num[i] / jnp.maximum(jnp.abs(den[i]), floor[i]) for i in idx]
    h_ms = [jnp.mean(hh[i] * hh[i], axis=-1, keepdims=True) for i in idx]
    for i, (bi, h) in enumerate(items):
        hn = hh[i] * lax.rsqrt(h_ms[i] + RMS_EPS) * gh_ref[:, sl[h]]
        hm_ref[bi:bi + 1, sl[h]] = jax.nn.sigmoid(o_ref[bi:bi + 1, sl[h]]) * hn
        c_ref[bi, h] = c_new[i]
        n_ref[bi, h:h + 1, :] = n_new[i]


def _mlstm_step(qc, kc, vm, om, gt, c0, n0, m0, b_gate_row, g_head):
    b = qc.shape[0]
    nb = SUBLANES
    hd = M_HEAD_DIM
    rows = lambda w: pl.BlockSpec((nb, w), lambda i: (i, 0))
    per_b = lambda shape: pl.BlockSpec((nb,) + shape, lambda i: (i,) + (0,) * len(shape))
    const = lambda shape: pl.BlockSpec(shape, lambda i: (0,) * len(shape))
    hm, c_new, n_new, m_new = pl.pallas_call(
        _mlstm_step_kernel,
        grid=(b // nb,),
        in_specs=[rows(M_WIDTH), rows(M_WIDTH), rows(M_WIDTH), rows(M_WIDTH), rows(LANES),
                  per_b((M_HEADS, hd, hd)), per_b((M_HEADS, hd)), rows(LANES),
                  const((1, LANES)), const((1, M_WIDTH))],
        out_specs=[rows(M_WIDTH), per_b((M_HEADS, hd, hd)), per_b((M_HEADS, hd)), rows(LANES)],
        out_shape=[jax.ShapeDtypeStruct((b, M_WIDTH), F32),
                   jax.ShapeDtypeStruct((b, M_HEADS, hd, hd), F32),
                   jax.ShapeDtypeStruct((b, M_HEADS, hd), F32),
                   jax.ShapeDtypeStruct((b, LANES), F32)],
        compiler_params=_cparams(("arbitrary",)),
        name="mlstm_sample",
    )(qc, kc, vm, om, gt, c0, n0, jnp.pad(m0, ((0, 0), (0, LANES - M_HEADS))), b_gate_row, g_head)
    return hm.astype(BF16), c_new, n_new, m_new[:, :M_HEADS]


def _sbp_kernel(bias_ref, q_ref, k_ref, v_ref, o_ref, carry_ref, acc_ref):
    blk = CHUNK
    tq = q_ref.shape[0]
    sub = tq // blk
    qi = pl.program_id(2)
    row = lax.broadcasted_iota(jnp.int32, (blk, blk), 0)
    col = lax.broadcasted_iota(jnp.int32, (blk, blk), 1)
    uo = jnp.concatenate([jnp.where(row > col, 1.0, 0.0), jnp.ones((blk, blk), F32)], axis=1).astype(BF16)
    uo2 = jnp.concatenate([uo, uo], axis=0)
    first_head = col < SB_HEAD_DIM
    bias2 = bias_ref[...]
    carry_ref[...] = jnp.zeros_like(carry_ref)
    acc_ref[...] = jnp.zeros_like(acc_ref)

    def head_pair_rows(x):
        zero = jnp.zeros_like(x)
        return jnp.concatenate([jnp.where(first_head, x, zero), jnp.where(first_head, zero, x)], axis=0)

    def tile(j, r0, diag):
        n = tq - r0
        ks = pl.ds(pl.multiple_of(j * blk, blk), blk)
        z = _dot_nt(q_ref[r0:tq, :], head_pair_rows(k_ref[ks, :])) + bias2
        ls, lstay = _log_sigmoid_parts(z)
        if diag:
            q_pos = qi * tq + r0 + lax.broadcasted_iota(jnp.int32, (n, 2 * blk), 0)
            k_pos = j * blk + (lax.broadcasted_iota(jnp.int32, (n, 2 * blk), 1) & (blk - 1))
            valid = k_pos < q_pos
            lstay = jnp.where(valid, lstay, 0.0)
        hi, lo = _split2(lstay)
        ra = _dot(jnp.concatenate([hi[:, :blk], lo[:, :blk]], axis=1), uo2)
        rb = _dot(jnp.concatenate([hi[:, blk:], lo[:, blk:]], axis=1), uo2)
        later = jnp.concatenate([ra[:, :blk], rb[:, :blk]], axis=1) + carry_ref[r0:tq, :]
        a = jnp.exp(ls + later)
        if diag:
            a = jnp.where(valid, a, 0.0)
        acc_ref[r0:tq, :] += _dot(a.astype(BF16), head_pair_rows(v_ref[ks, :]))
        carry_ref[r0:tq, :] += jnp.concatenate([ra[:, blk:], rb[:, blk:]], axis=1)

    for m in reversed(range(sub)):
        tile(qi * sub + m, m * blk, True)

    unroll = 2 if sub % 2 == 0 else 1

    def body(t, c):
        for s in range(unroll):
            tile(qi * sub - 1 - (t * unroll + s), 0, False)
        return c

    lax.fori_loop(0, qi * sub // unroll, body, 0)
    o_ref[...] = acc_ref[...].astype(BF16)


def _sb_prompt(q, k, v, bias):
    b, t, w = q.shape
    tq = min(SBQ, t)
    pairs = w // LANES
    per_pair = LANES // SB_HEAD_DIM
    bias2 = jnp.repeat(bias.reshape(pairs, per_pair), CHUNK, axis=1).reshape(pairs, 1, per_pair * CHUNK)
    return pl.pallas_call(
        _sbp_kernel,
        grid=(b, pairs, t // tq),
        in_specs=[pl.BlockSpec((None, 1, per_pair * CHUNK), lambda i, h, s: (h, 0, 0)),
                  pl.BlockSpec((None, tq, LANES), lambda i, h, s: (i, s, h)),
                  pl.BlockSpec((None, t, LANES), lambda i, h, s: (i, 0, h)),
                  pl.BlockSpec((None, t, LANES), lambda i, h, s: (i, 0, h))],
        out_specs=pl.BlockSpec((None, tq, LANES), lambda i, h, s: (i, s, h)),
        out_shape=jax.ShapeDtypeStruct((b, t, w), BF16),
        scratch_shapes=[pltpu.VMEM((tq, per_pair * CHUNK), F32), pltpu.VMEM((tq, LANES), F32)],
        compiler_params=_cparams(("arbitrary", "arbitrary", "arbitrary")),
        name="sb_prompt",
    )(bias2, q, k, v)


def _sbs_group(first, bias_ref, qb_ref, k_refs, v_refs, carry_ref, z_ref, a_ref, acc_ref):
    G = len(k_refs)
    H = SB_HEADS
    page = k_refs[0].shape[-1]
    for g in range(G):
        for h in range(H):
            z_ref[g * H + h:g * H + h + 1, :] = jnp.sum(k_refs[g][h] * qb_ref[h], axis=0, keepdims=True)
    ls, lstay = _log_sigmoid_parts(z_ref[...] + bias_ref[...])

    row = lax.broadcasted_iota(jnp.int32, (page, page), 0)
    col = lax.broadcasted_iota(jnp.int32, (page, page), 1)
    uo = jnp.concatenate([jnp.where(row > col, 1.0, 0.0), jnp.ones((page, page), F32)], axis=1).astype(BF16)
    hi, lo = _split2(lstay)
    r = _dot(jnp.concatenate([hi, lo], axis=1), jnp.concatenate([uo, uo], axis=0))
    within, tot = r[:, :page], r[:, page:]
    n = G * H
    rr = lax.broadcasted_iota(jnp.int32, (n, n), 0)
    rc = lax.broadcasted_iota(jnp.int32, (n, n), 1)
    shift = H.bit_length() - 1
    later_page = ((rr & (H - 1)) == (rc & (H - 1))) & ((rc >> shift) > (rr >> shift))
    mx = jnp.where(later_page, 1.0, 0.0).astype(BF16)
    thi, tlo = _split2(tot)
    carry = jnp.where(first, 0.0, carry_ref[...])
    a_ref[...] = jnp.exp(ls + within + _dot(mx, thi) + _dot(mx, tlo) + jnp.tile(carry, (G, 1)))
    for g in range(G):
        carry = carry + tot[g * H:(g + 1) * H, :]
    carry_ref[...] = carry

    for h in range(H):
        acc = jnp.where(first, 0.0, acc_ref[h])
        for g in range(G):
            acc = acc + v_refs[g][h] * a_ref[g * H + h:g * H + h + 1, :]
        acc_ref[h] = acc


def _sbs_finish(acc_ref, o_ref):
    page = acc_ref.shape[-1]
    ones = jnp.ones((SUBLANES, page), BF16)
    for h in range(SB_HEADS):
        x = acc_ref[h]
        xh = x.astype(BF16)
        r1 = x - xh.astype(F32)
        xm = r1.astype(BF16)
        xl = (r1 - xm.astype(F32)).astype(BF16)
        s = _dot_nt(ones, xh) + _dot_nt(ones, xm) + _dot_nt(ones, xl)
        o_ref[h:h + 1, :] = s[0:1, :]


def _sbs_kernel(pt_ref, bias_ref, qb_ref, *refs):
    del pt_ref
    G = PAGES_PER_STEP
    k_refs, v_refs = refs[:G], refs[G:2 * G]
    o_ref, carry_ref, z_ref, a_ref, acc_ref = refs[2 * G:]
    j = pl.program_id(1)
    _sbs_group(j == 0, bias_ref, qb_ref, k_refs, v_refs, carry_ref, z_ref, a_ref, acc_ref)

    @pl.when(j == pl.num_programs(1) - 1)
    def _():
        _sbs_finish(acc_ref, o_ref)


def _sb_sample(layer, q_bf, cache_k, cache_v, page_table, bias):
    b = q_bf.shape[0]
    n_pages = page_table.shape[1]
    page = cache_k.shape[2]
    G = PAGES_PER_STEP
    H = SB_HEADS
    groups = n_pages // G
    kt = jnp.transpose(cache_k, (0, 1, 3, 4, 2))
    vt = jnp.transpose(cache_v, (0, 1, 3, 4, 2))
    qb = jnp.broadcast_to(q_bf.astype(F32).reshape(b, H, SB_HEAD_DIM, 1), (b, H, SB_HEAD_DIM, page))
    bias_rows = jnp.broadcast_to(jnp.tile(bias, G).reshape(G * H, 1), (G * H, page))

    def page_spec(g):
        return pl.BlockSpec(
            (None, None, H, SB_HEAD_DIM, page),
            lambda i, j, pt: (layer, pt[i, (groups - 1 - j) * G + g], 0, 0, 0))

    out = pl.pallas_call(
        _sbs_kernel,
        grid_spec=pltpu.PrefetchScalarGridSpec(
            num_scalar_prefetch=1,
            grid=(b, groups),
            in_specs=[pl.BlockSpec((G * H, page), lambda i, j, pt: (0, 0)),
                      pl.BlockSpec((None, H, SB_HEAD_DIM, page), lambda i, j, pt: (i, 0, 0, 0))]
                     + [page_spec(g) for g in range(G)] * 2,
            out_specs=pl.BlockSpec((None, H, SB_HEAD_DIM), lambda i, j, pt: (i, 0, 0)),
            scratch_shapes=[pltpu.VMEM((H, page), F32), pltpu.VMEM((G * H, page), F32),
                            pltpu.VMEM((G * H, page), F32), pltpu.VMEM((H, SB_HEAD_DIM, page), F32)],
        ),
        out_shape=jax.ShapeDtypeStruct((b, H, SB_HEAD_DIM), F32),
        compiler_params=_cparams(("arbitrary", "arbitrary")),
        name="sb_sample",
    )(page_table, bias_rows, qb, *([kt] * G), *([vt] * G))
    return out.reshape(b, SB_WIDTH).astype(BF16)


def _ffn_kernel(x_ref, hm_ref, hs_ref, ga1_ref, sh2_ref, sc2_ref, ga2_ref, g2_ref,
                wo_ref, wg_ref, wu_ref, wd_ref, *rest, final):
    if final:
        gf_ref, shf_ref, scf_ref, o_ref = rest
    else:
        o_ref, = rest
    attn = _dot(hm_ref[...], wo_ref[0:M_WIDTH, :]) + _dot(hs_ref[...], wo_ref[M_WIDTH:, :])
    x1 = x_ref[...] + ga1_ref[...] * attn
    h2 = _rms_mod(x1, g2_ref[...], sc2_ref[...], sh2_ref[...]).astype(BF16)
    gate = _dot(h2, wg_ref[...])
    up = _dot(h2, wu_ref[...])
    act = (gate * jax.nn.sigmoid(gate) * up).astype(BF16)
    x2 = x1 + ga2_ref[...] * _dot(act, wd_ref[...])
    if final:
        x2 = _rms_mod(x2, gf_ref[...], scf_ref[...], shf_ref[...])
    o_ref[...] = x2


def _ffn(x2, hm, hs, mod, g2, wo, wg, wu, wd, group, tm, seq, n_sample, final=None):
    n, d = x2.shape
    mods = [_mod_operand(mod, c, group, tm, seq, n_sample, d) for c in (2, 3, 4, 5)]
    row = lambda w: pl.BlockSpec((tm, w), lambda i: (i, 0))
    vec = pl.BlockSpec((1, d), lambda i: (0, 0))
    resident = lambda w: pl.BlockSpec(w.shape, lambda i: (0, 0), pipeline_mode=pl.Buffered(1))
    operands = [x2, hm, hs] + [a for a, _ in mods] + [g2, wo, wg, wu, wd]
    in_specs = ([row(d), row(M_WIDTH), row(SB_WIDTH)] + [s for _, s in mods]
                + [vec, resident(wo), resident(wg), resident(wu), resident(wd)])
    if final is not None:
        mod_f, g_f = final
        fmods = [_mod_operand(mod_f, c, group, tm, seq, n_sample, d) for c in (0, 1)]
        operands += [g_f] + [a for a, _ in fmods]
        in_specs += [vec] + [s for _, s in fmods]
    return pl.pallas_call(
        functools.partial(_ffn_kernel, final=final is not None),
        grid=(n // tm,),
        in_specs=in_specs,
        out_specs=row(d),
        out_shape=jax.ShapeDtypeStruct((n, d), F32),
        compiler_params=_cparams(("arbitrary",)),
        name="ffn_" + group,
    )(*operands)


def kernel(x_prompt, x_sample, cache_sb_k, cache_sb_v, state_conv, state_mlstm_C, state_mlstm_n,
           state_mlstm_m, page_table, c_prompt, c_sample, w_ada, b_ada, norm1_g, w_in, b_gate, sb_bias,
           conv_w, conv_b, head_norm_g, w_out, norm2_g, w_gate, w_up, w_down, w_ada_f, b_ada_f, norm_f_g):
    bp, seq, d = x_prompt.shape
    bs = x_sample.shape[0]
    depth = w_ada.shape[0]
    n_gate = 2 * M_HEADS
    assert x_sample.shape[1] == 1 and seq % CHUNK == 0
    assert page_table.shape[1] % PAGES_PER_STEP == 0

    c_all = jnp.concatenate([c_sample, c_prompt], axis=0)
    xp = x_prompt.reshape(bp * seq, d)
    xs = x_sample.reshape(bs, d)
    tm_p = min(1024, seq)
    tm_f = min(512, seq)
    assert seq % tm_p == 0 and seq % tm_f == 0

    assert bp % MLSTM_SEQS_PER_STEP == 0 and bs % SUBLANES == 0
    zeros_c = jnp.zeros((bp, M_HEADS, M_HEAD_DIM, M_HEAD_DIM), F32)
    zeros_n = jnp.zeros((bp, M_HEADS, M_HEAD_DIM), F32)
    zeros_m = jnp.zeros((bp, 1, M_HEADS), F32)

    outs = {k: [] for k in ("kp", "vp", "ks", "vs", "cvp", "cvs", "Cp", "Cs", "np", "ns", "mp", "ms")}
    mod_f = _ada(c_all, w_ada_f[None], b_ada_f[None], 0)
    gf = norm_f_g.reshape(1, d)
    for l in range(depth):
        mod = _ada(c_all, w_ada, b_ada, l)
        wt = _pack_w_in(w_in[l])
        wo, wg, wu, wd = (w_out[l].astype(BF16), w_gate[l].astype(BF16), w_up[l].astype(BF16),
                          w_down[l].astype(BF16))
        g1 = norm1_g[l].reshape(1, d)
        g2 = norm2_g[l].reshape(1, d)
        bg_row = jnp.pad(b_gate[l], (0, LANES - n_gate)).reshape(1, LANES)
        cb = conv_b[l].reshape(1, 2 * M_WIDTH)
        gh = head_norm_g[l].reshape(1, M_WIDTH)
        final = (mod_f, gf) if l == depth - 1 else None

        qc, kc, vm, om, gt, q, kb, vb, kt, vt, cs = _inproj_prompt(
            xp, mod, g1, wt, conv_w[l], cb, tm_p, seq, bs)
        r3 = lambda a: a.reshape(bp, seq, a.shape[-1])
        hm, c_new, n_new, m_new = _mlstm(r3(qc), r3(kc), r3(vm), r3(om), r3(gt), zeros_c, zeros_n,
                                         zeros_m, bg_row, gh)
        hs = _sb_prompt(r3(q), r3(kb), r3(vb), sb_bias[l])
        xp = _ffn(xp, hm.reshape(bp * seq, M_WIDTH), hs.reshape(bp * seq, SB_WIDTH), mod, g2,
                  wo, wg, wu, wd, "prompt", tm_f, seq, bs, final)
        outs["kp"].append(kt)
        outs["vp"].append(vt)
        outs["cvp"].append(cs[:, SUBLANES - (CONV_W - 1):, :])
        outs["Cp"].append(c_new)
        outs["np"].append(n_new)
        outs["mp"].append(m_new.reshape(bp, M_HEADS))

        qk, qc, kc, vm, om, gt, q, kf, vf = _inproj_sample(xs, mod, g1, wt, conv_w[l], cb, state_conv[l])
        hm, c_new, n_new, m_new = _mlstm_step(qc, kc, vm, om, gt, state_mlstm_C[l], state_mlstm_n[l],
                                              state_mlstm_m[l], bg_row, gh)
        hs = _sb_sample(l, q, cache_sb_k, cache_sb_v, page_table, sb_bias[l])
        xs = _ffn(xs, hm, hs, mod, g2, wo, wg, wu, wd, "sample", bs, 1, bs, final)
        outs["ks"].append(kf.reshape(bs, 1, SB_HEADS, SB_HEAD_DIM))
        outs["vs"].append(vf.reshape(bs, 1, SB_HEADS, SB_HEAD_DIM))
        outs["cvs"].append(jnp.concatenate([state_conv[l][:, 1:, :], qk[:, None, :]], axis=1))
        outs["Cs"].append(c_new)
        outs["ns"].append(n_new)
        outs["ms"].append(m_new.reshape(bs, M_HEADS))

    y_prompt = xp.reshape(bp, seq, d)
    y_sample = xs.reshape(bs, 1, d)
    st = lambda k: jnp.stack(outs[k])
    tok_major = lambda a: jnp.transpose(a.reshape(depth, bp, SB_HEADS, SB_HEAD_DIM, seq), (0, 1, 4, 2, 3))
    return (y_prompt, y_sample, tok_major(st("kp")), tok_major(st("vp")), st("ks"), st("vs"),
            st("cvp"), st("cvs"), st("Cp"), st("Cs"), st("np"), st("ns"), st("mp"), st("ms"))
```

```python
import functools

import jax
import jax.numpy as jnp
from jax import lax
from jax.experimental import pallas as pl
from jax.experimental.pallas import tpu as pltpu

F32 = jnp.float32
BF16 = jnp.bfloat16

RMS_EPS = 1e-6
M_HEADS = 4
M_HEAD_DIM = 128
M_WIDTH = M_HEADS * M_HEAD_DIM
SB_HEADS = 8
SB_HEAD_DIM = 64
SB_WIDTH = SB_HEADS * SB_HEAD_DIM
CONV_W = 4
CHUNK = 128
LANES = 128
SUBLANES = 8
PAGES_PER_STEP = 16
SBQ = 512
MLSTM_SEQS_PER_STEP = 8
LOG2_E = 1.4426950408889634
NEG_BIG = -1e30
VMEM_LIMIT = 56 * 1024 * 1024


def _cparams(sem):
    return pltpu.CompilerParams(dimension_semantics=sem, vmem_limit_bytes=VMEM_LIMIT)


def _log_sigmoid_parts(z):
    l1p = jnp.log(1.0 + jnp.exp2(jnp.abs(z) * (-LOG2_E)))
    ls = jnp.minimum(z, 0.0) - l1p
    return ls, ls - z


def _split2(x):
    hi = x.astype(BF16)
    lo = (x - hi.astype(F32)).astype(BF16)
    return hi, lo


def _rms_mod(x, g, sc, sh):
    y = x * lax.rsqrt(jnp.mean(x * x, axis=-1, keepdims=True) + RMS_EPS)
    return (y * g) * (1.0 + sc) + sh


def _dot(a, b):
    return jnp.dot(a, b, preferred_element_type=F32)


def _dot_nt(a, b):
    return lax.dot_general(a, b, (((1,), (1,)), ((), ())), preferred_element_type=F32)


def _ada_kernel(c_ref, w_ref, b_ref, o_ref):
    c = c_ref[...]
    a = (c * jax.nn.sigmoid(c)).astype(BF16)
    o_ref[...] = _dot(a, w_ref[...].astype(BF16)) + b_ref[...]


def _ada(c_all, w, b, layer):
    m, d = c_all.shape
    n = w.shape[2]
    tn = 1024
    return pl.pallas_call(
        _ada_kernel,
        grid=(n // tn,),
        in_specs=[pl.BlockSpec((m, d), lambda j: (0, 0)),
                  pl.BlockSpec((None, d, tn), lambda j: (layer, 0, j)),
                  pl.BlockSpec((None, 1, tn), lambda j: (layer, 0, j))],
        out_specs=pl.BlockSpec((m, tn), lambda j: (0, j)),
        out_shape=jax.ShapeDtypeStruct((m, n), F32),
        compiler_params=_cparams(("arbitrary",)),
        name="ada_mod",
    )(c_all, w, b.reshape(b.shape[0], 1, n))


def _mod_operand(mod, chunk, group, tm, seq, n_sample, d):
    if group == "sample":
        return mod, pl.BlockSpec((n_sample, d), lambda i, *_: (0, chunk))
    mod3 = mod.reshape(mod.shape[0], 1, mod.shape[1])
    return mod3, pl.BlockSpec((None, 1, d), lambda i, *_: (n_sample + (i * tm) // seq, 0, chunk))


_SEG_QK = (0, 2 * M_WIDTH)
_SEG_VM = (_SEG_QK[1], _SEG_QK[1] + M_WIDTH)
_SEG_OM = (_SEG_VM[1], _SEG_VM[1] + M_WIDTH)
_SEG_Q = (_SEG_OM[1], _SEG_OM[1] + SB_WIDTH)
_SEG_K = (_SEG_Q[1], _SEG_Q[1] + SB_WIDTH)
_SEG_V = (_SEG_K[1], _SEG_K[1] + SB_WIDTH)
_SEG_GATES = (_SEG_V[1], _SEG_V[1] + LANES)


def _pack_w_in(w_in_l):
    n_gate = 2 * M_HEADS
    g0 = 4 * M_WIDTH
    wt = w_in_l.T
    return jnp.concatenate(
        [wt[:g0], wt[g0 + n_gate:], jnp.pad(wt[g0:g0 + n_gate], ((0, LANES - n_gate), (0, 0)))],
        axis=0).astype(BF16)


def _conv_silu_split(acc, qc_ref, kc_ref):
    qkc = acc * jax.nn.sigmoid(acc)
    qc_ref[...] = qkc[:, :M_WIDTH]
    kc_ref[...] = qkc[:, M_WIDTH:] * (M_HEAD_DIM ** -0.5)


def _inproj_prompt_kernel(x_ref, g_ref, sc_ref, sh_ref, wt_ref, cw_ref, cb_ref, kt_all_ref, vt_all_ref,
                          qc_ref, kc_ref, vm_ref, om_ref, gt_ref, q_ref, kb_ref, vb_ref,
                          kt_ref, vt_ref, cs_ref, ubuf, *, tiles_per_seq):
    del kt_all_ref, vt_all_ref
    tm = x_ref.shape[0]
    h = _rms_mod(x_ref[...], g_ref[...], sc_ref[...], sh_ref[...]).astype(BF16)
    seg = lambda s: _dot_nt(h, wt_ref[s[0]:s[1], :])

    @pl.when(pl.program_id(0) % tiles_per_seq == 0)
    def _():
        ubuf[0:SUBLANES, :] = jnp.zeros((SUBLANES, ubuf.shape[1]), F32)

    def conv_silu(c0, c1, dst_ref, d0, scale):
        cols = slice(c0, c1)
        u = _dot_nt(h, wt_ref[c0:c1, :])
        ubuf[SUBLANES:SUBLANES + tm, cols] = u
        acc = cb_ref[:, cols] + ubuf[5:5 + tm, cols] * cw_ref[0:1, cols]
        acc = acc + ubuf[6:6 + tm, cols] * cw_ref[1:2, cols]
        acc = acc + ubuf[7:7 + tm, cols] * cw_ref[2:3, cols]
        acc = acc + u * cw_ref[3:4, cols]
        last = ubuf[tm:tm + SUBLANES, cols]
        ubuf[0:SUBLANES, cols] = last
        cs_ref[:, cols] = last
        y = acc * jax.nn.sigmoid(acc)
        if scale is not None:
            y = y * scale
        dst_ref[:, d0:d0 + (c1 - c0)] = y.astype(BF16)

    half = M_WIDTH // 2
    conv_silu(0, half, qc_ref, 0, None)
    vm_ref[...] = seg(_SEG_VM).astype(BF16)
    conv_silu(half, M_WIDTH, qc_ref, half, None)
    om_ref[...] = seg(_SEG_OM)
    conv_silu(M_WIDTH, M_WIDTH + half, kc_ref, 0, M_HEAD_DIM ** -0.5)
    q_ref[...] = (seg(_SEG_Q) * (SB_HEAD_DIM ** -0.5)).astype(BF16)
    conv_silu(M_WIDTH + half, 2 * M_WIDTH, kc_ref, half, M_HEAD_DIM ** -0.5)
    gt_ref[...] = seg(_SEG_GATES)
    k = seg(_SEG_K)
    kb_ref[...] = k.astype(BF16)
    kt_ref[...] = k.T
    v = seg(_SEG_V)
    vb_ref[...] = v.astype(BF16)
    vt_ref[...] = v.T


def _inproj_prompt(x2, mod, g1, wt, conv_w, conv_b, kt_all, vt_all, layer, tm, seq, n_sample):
    n, d = x2.shape
    b = n // seq
    tps = seq // tm
    sh_a, sh_s = _mod_operand(mod, 0, "prompt", tm, seq, n_sample, d)
    sc_a, sc_s = _mod_operand(mod, 1, "prompt", tm, seq, n_sample, d)
    row = lambda w: pl.BlockSpec((tm, w), lambda i: (i, 0))
    const = lambda a: pl.BlockSpec(a.shape, lambda i: (0,) * a.ndim)
    tr = pl.BlockSpec((None, None, SB_WIDTH, tm), lambda i: (layer, i // tps, 0, i % tps))
    hbm = pl.BlockSpec(memory_space=pl.ANY)
    rows = [(M_WIDTH, BF16), (M_WIDTH, BF16), (M_WIDTH, BF16), (M_WIDTH, F32), (LANES, F32),
            (SB_WIDTH, BF16), (SB_WIDTH, BF16), (SB_WIDTH, BF16)]
    operands = (x2, g1, sc_a, sh_a, wt, conv_w, conv_b, kt_all, vt_all)
    return pl.pallas_call(
        functools.partial(_inproj_prompt_kernel, tiles_per_seq=tps),
        grid=(n // tm,),
        in_specs=[row(d), const(g1), sc_s, sh_s,
                  pl.BlockSpec(wt.shape, lambda i: (0, 0), pipeline_mode=pl.Buffered(1)),
                  const(conv_w), const(conv_b), hbm, hbm],
        out_specs=[row(w) for w, _ in rows] + [tr, tr,
                   pl.BlockSpec((None, SUBLANES, 2 * M_WIDTH), lambda i: (i // tps, 0, 0))],
        out_shape=[jax.ShapeDtypeStruct((n, w), dt) for w, dt in rows]
                  + [jax.ShapeDtypeStruct(kt_all.shape, F32), jax.ShapeDtypeStruct(vt_all.shape, F32)]
                  + [jax.ShapeDtypeStruct((b, SUBLANES, 2 * M_WIDTH), F32)],
        input_output_aliases={len(operands) - 2: len(rows), len(operands) - 1: len(rows) + 1},
        scratch_shapes=[pltpu.VMEM((SUBLANES + tm, 2 * M_WIDTH), F32)],
        compiler_params=_cparams(("arbitrary",)),
        name="in_proj_prompt",
    )(*operands)


def _inproj_sample_kernel(x_ref, g_ref, sc_ref, sh_ref, wt_ref, cw_ref, cb_ref, s0_ref, s1_ref, s2_ref,
                          qk_ref, qc_ref, kc_ref, vm_ref, om_ref, gt_ref, q_ref, kf_ref, vf_ref):
    h = _rms_mod(x_ref[...], g_ref[...], sc_ref[...], sh_ref[...]).astype(BF16)
    seg = lambda s: _dot_nt(h, wt_ref[s[0]:s[1], :])
    u = seg(_SEG_QK)
    qk_ref[...] = u
    cw = cw_ref[...]
    acc = cb_ref[...] + s0_ref[...] * cw[0:1, :]
    acc = acc + s1_ref[...] * cw[1:2, :]
    acc = acc + s2_ref[...] * cw[2:3, :]
    acc = acc + u * cw[3:4, :]
    _conv_silu_split(acc, qc_ref, kc_ref)
    vm_ref[...] = seg(_SEG_VM)
    om_ref[...] = seg(_SEG_OM)
    gt_ref[...] = seg(_SEG_GATES)
    q_ref[...] = (seg(_SEG_Q) * (SB_HEAD_DIM ** -0.5)).astype(BF16)
    kf_ref[...] = seg(_SEG_K)
    vf_ref[...] = seg(_SEG_V)


def _inproj_sample(x2, mod, g1, wt, conv_w, conv_b, conv_state):
    n, d = x2.shape
    sh_a, sh_s = _mod_operand(mod, 0, "sample", n, 1, n, d)
    sc_a, sc_s = _mod_operand(mod, 1, "sample", n, 1, n, d)
    full = lambda a: pl.BlockSpec(a.shape, lambda i: (0,) * a.ndim)
    states = [conv_state[:, r, :] for r in range(CONV_W - 1)]
    outs = [(2 * M_WIDTH, F32), (M_WIDTH, F32), (M_WIDTH, F32), (M_WIDTH, F32), (M_WIDTH, F32),
            (LANES, F32), (SB_WIDTH, BF16), (SB_WIDTH, F32), (SB_WIDTH, F32)]
    return pl.pallas_call(
        _inproj_sample_kernel,
        grid=(1,),
        in_specs=[full(x2), full(g1), sc_s, sh_s, full(wt), full(conv_w), full(conv_b)]
                 + [full(s) for s in states],
        out_specs=[pl.BlockSpec((n, w), lambda i: (0, 0)) for w, _ in outs],
        out_shape=[jax.ShapeDtypeStruct((n, w), dt) for w, dt in outs],
        compiler_params=_cparams(("arbitrary",)),
        name="in_proj_sample",
    )(x2, g1, sc_a, sh_a, wt, conv_w, conv_b, *states)


def _mlstm_kernel(q_ref, k_ref, v_ref, o_ref, gt_ref, c0_ref, n0_ref, m0_ref, bg_ref, gh_ref,
                  hm_ref, c_ref, n_ref, m_ref):
    L = CHUNK
    nb = q_ref.shape[0]
    ci = pl.program_id(1)

    @pl.when(ci == 0)
    def _():
        c_ref[...] = c0_ref[...]
        n_ref[...] = n0_ref[...]
        m_ref[...] = m0_ref[...]

    row = lax.broadcasted_iota(jnp.int32, (L, L), 0)
    col = lax.broadcasted_iota(jnp.int32, (L, L), 1)
    tri = row >= col
    tril = jnp.where(tri, 1.0, 0.0).astype(BF16)
    lane = lax.broadcasted_iota(jnp.int32, (L, LANES), 1)
    head_lane = lax.broadcasted_iota(jnp.int32, (1, M_HEADS), 1)
    src = lax.broadcasted_iota(jnp.int32, (3 * LANES, 2 * LANES), 0) & (LANES - 1)
    dst = lax.broadcasted_iota(jnp.int32, (3 * LANES, 2 * LANES), 1)
    sel = [jnp.where(src == jnp.where(dst < LANES, M_HEADS + h, h), 1.0, 0.0).astype(BF16)
           for h in range(M_HEADS)]
    n_in = [n_ref[bi] for bi in range(nb)]
    m_in = [m_ref[bi] for bi in range(nb)]
    stores = []

    items = [(bi, h) for bi in range(nb) for h in range(M_HEADS)]
    gates = []
    for bi in range(nb):
        gf = gt_ref[bi] + bg_ref[...]
        lsg, _ = _log_sigmoid_parts(gf)
        li_all = gf
        hi = lsg.astype(BF16)
        r1 = lsg - hi.astype(F32)
        mid = r1.astype(BF16)
        lo = (r1 - mid.astype(F32)).astype(BF16)
        bcum = _dot(tril, hi) + _dot(tril, mid) + _dot(tril, lo)
        gmat = jnp.where(lane < M_HEADS, li_all, bcum)
        gmat_t = gmat.T
        ghi = gmat.astype(BF16)
        g1 = gmat - ghi.astype(F32)
        gmid = g1.astype(BF16)
        g3 = jnp.concatenate([ghi, gmid, (g1 - gmid.astype(F32)).astype(BF16)], axis=1)
        gates.append((gmat_t, g3))

    sl = [slice(h * M_HEAD_DIM, (h + 1) * M_HEAD_DIM) for h in range(M_HEADS)]
    qs = [q_ref[bi, :, sl[h]] for bi, h in items]
    ks = [k_ref[bi, :, sl[h]] for bi, h in items]
    vs = [v_ref[bi, :, sl[h]] for bi, h in items]
    cs = [c_ref[bi, h] for bi, h in items]
    ns = [n_in[bi][h:h + 1, :] for bi, h in items]
    li_row = [gates[bi][0][h:h + 1, :] for bi, h in items]
    b_row = [gates[bi][0][M_HEADS + h:M_HEADS + h + 1, :] for bi, h in items]
    m_prev = [jnp.sum(jnp.where(head_lane == h, m_in[bi], 0.0), axis=-1, keepdims=True) for bi, h in items]
    idx = range(len(items))

    rep = [_dot(gates[bi][1], sel[h]) for bi, h in items]
    b_col = [r[:, :LANES] for r in rep]
    li_col = [r[:, LANES:] for r in rep]
    s_qk = [_dot_nt(qs[i], ks[i]) for i in idx]
    q_c = [_dot(qs[i], cs[i].astype(BF16)) for i in idx]
    q_n = []
    for i in idx:
        nhi, nlo = _split2(jnp.broadcast_to(ns[i], (M_HEAD_DIM, M_HEAD_DIM)))
        q_n.append(_dot_nt(qs[i], nhi) + _dot_nt(qs[i], nlo))

    dmat = [jnp.where(tri, b_col[i] - b_row[i] + li_row[i], NEG_BIG) for i in idx]
    row_max = [jnp.max(dmat[i], axis=-1, keepdims=True) for i in idx]
    m_inter = [b_col[i] + m_prev[i] for i in idx]
    m_t = [jnp.maximum(m_inter[i], row_max[i]) for i in idx]
    w_inter = [jnp.exp(m_inter[i] - m_t[i]) for i in idx]
    w_intra = [jnp.exp(dmat[i] - m_t[i]) * s_qk[i] for i in idx]
    w_sum = [jnp.sum(w_intra[i], axis=-1, keepdims=True) for i in idx]
    w_v = [_dot(w_intra[i].astype(BF16), vs[i]) for i in idx]
    hh = []
    for i in idx:
        num = w_inter[i] * q_c[i] + w_v[i]
        den = w_inter[i] * q_n[i] + w_sum[i]
        hh.append(num / jnp.maximum(jnp.abs(den), jnp.exp(-m_t[i])))
    h_ms = [jnp.mean(hh[i] * hh[i], axis=-1, keepdims=True) for i in idx]
    for i, (bi, h) in enumerate(items):
        hn = hh[i] * lax.rsqrt(h_ms[i] + RMS_EPS) * gh_ref[:, sl[h]]
        stores.append((hm_ref, (bi, slice(None), sl[h]),
                       (jax.nn.sigmoid(o_ref[bi, :, sl[h]]) * hn).astype(BF16)))

    b_last = [b_col[i][L - 1:L, :] for i in idx]
    g_max = [jnp.max(b_last[i] - b_row[i] + li_row[i], axis=-1, keepdims=True) for i in idx]
    m_new = [jnp.maximum(b_last[i] + m_prev[i], g_max[i]) for i in idx]
    kg = [ks[i].astype(F32) * jnp.exp(b_last[i] - b_col[i] + li_col[i] - m_new[i]) for i in idx]
    decay = [jnp.exp(b_last[i] + m_prev[i] - m_new[i]) for i in idx]
    kg_t = [kg[i].T.astype(BF16) for i in idx]
    for i, (bi, h) in enumerate(items):
        stores.append((c_ref, (bi, h), decay[i] * cs[i] + _dot(kg_t[i], vs[i])))
    for bi in range(nb):
        mine = [i for i in idx if items[i][0] == bi]
        n_rows = [decay[i] * ns[i] + jnp.sum(kg[i], axis=0, keepdims=True) for i in mine]
        m_out = m_in[bi]
        for i in mine:
            m_out = jnp.where(head_lane == items[i][1], m_new[i][:, :M_HEADS], m_out)
        stores.append((n_ref, (bi,), jnp.concatenate(n_rows, axis=0)))
        stores.append((m_ref, (bi,), m_out))
    for ref, where, val in stores:
        ref[where] = val


def _mlstm(qc, kc, vm, om, gt, c0, n0, m0, b_gate_row, g_head):
    b, t, _ = qc.shape
    nc = t // CHUNK
    nb = MLSTM_SEQS_PER_STEP
    tok = lambda w: pl.BlockSpec((nb, CHUNK, w), lambda i, c: (i, c, 0))
    per_b = lambda shape: pl.BlockSpec((nb,) + shape, lambda i, c: (i,) + (0,) * len(shape))
    const = lambda shape: pl.BlockSpec(shape, lambda i, c: (0,) * len(shape))
    hd = M_HEAD_DIM
    return pl.pallas_call(
        _mlstm_kernel,
        grid=(b // nb, nc),
        in_specs=[tok(M_WIDTH), tok(M_WIDTH), tok(M_WIDTH), tok(M_WIDTH), tok(LANES),
                  per_b((M_HEADS, hd, hd)), per_b((M_HEADS, hd)), per_b((1, M_HEADS)),
                  const((1, LANES)), const((1, M_WIDTH))],
        out_specs=[tok(M_WIDTH), per_b((M_HEADS, hd, hd)), per_b((M_HEADS, hd)), per_b((1, M_HEADS))],
        out_shape=[jax.ShapeDtypeStruct((b, t, M_WIDTH), BF16),
                   jax.ShapeDtypeStruct((b, M_HEADS, hd, hd), F32),
                   jax.ShapeDtypeStruct((b, M_HEADS, hd), F32),
                   jax.ShapeDtypeStruct((b, 1, M_HEADS), F32)],
        compiler_params=_cparams(("arbitrary", "arbitrary")),
        name="mlstm_prompt",
    )(qc, kc, vm, om, gt, c0, n0, m0, b_gate_row, g_head)


def _mlstm_step_kernel(q_ref, k_ref, v_ref, o_ref, gt_ref, c0_ref, n0_ref, m0_ref, bg_ref, gh_ref,
                       hm_ref, c_ref, n_ref, m_ref):
    nb = q_ref.shape[0]
    hd = M_HEAD_DIM
    gf = gt_ref[...] + bg_ref[...]
    lsg, _ = _log_sigmoid_parts(gf)
    lf = pltpu.roll(lsg, LANES - M_HEADS, axis=1)
    m_old = m0_ref[...]
    m_new = jnp.maximum(lf + m_old, gf)
    decay_all = jnp.exp(lf + m_old - m_new)
    gain_all = jnp.exp(gf - m_new)
    floor_all = jnp.exp(-m_new)
    m_ref[...] = m_new

    lane = lax.broadcasted_iota(jnp.int32, (1, LANES), 1)

    def pick(x, bi, h):
        return jnp.sum(jnp.where(lane == h, x[bi:bi + 1, :], 0.0), axis=-1, keepdims=True)

    items = [(bi, h) for bi in range(nb) for h in range(M_HEADS)]
    sl = [slice(h * hd, (h + 1) * hd) for h in range(M_HEADS)]
    idx = range(len(items))
    decay = [pick(decay_all, bi, h) for bi, h in items]
    gain = [pick(gain_all, bi, h) for bi, h in items]
    floor = [pick(floor_all, bi, h) for bi, h in items]
    qs = [q_ref[bi:bi + 1, sl[h]] for bi, h in items]
    vs = [v_ref[bi:bi + 1, sl[h]] for bi, h in items]
    kg = [k_ref[bi:bi + 1, sl[h]] * gain[i] for i, (bi, h) in enumerate(items)]
    kg_col = [jnp.broadcast_to(kg[i], (hd, hd)).T for i in idx]
    c_new = [decay[i] * c0_ref[bi, h] + kg_col[i] * vs[i] for i, (bi, h) in enumerate(items)]
    n_new = [decay[i] * n0_ref[bi, h:h + 1, :] + kg[i] for i, (bi, h) in enumerate(items)]
    num = [_dot(jnp.broadcast_to(qs[i], (SUBLANES, hd)).astype(BF16), c_new[i].astype(BF16))[0:1, :]
           for i in idx]
    den = [jnp.sum(qs[i] * n_new[i], axis=-1, keepdims=True) for i in idx]
    hh = [num[i] / jnp.maximum(jnp.abs(den[i]), floor[i]) for i in idx]
    h_ms = [jnp.mean(hh[i] * hh[i], axis=-1, keepdims=True) for i in idx]
    for i, (bi, h) in enumerate(items):
        hn = hh[i] * lax.rsqrt(h_ms[i] + RMS_EPS) * gh_ref[:, sl[h]]
        hm_ref[bi:bi + 1, sl[h]] = jax.nn.sigmoid(o_ref[bi:bi + 1, sl[h]]) * hn
        c_ref[bi, h] = c_new[i]
        n_ref[bi, h:h + 1, :] = n_new[i]


def _mlstm_step(qc, kc, vm, om, gt, c0, n0, m0, b_gate_row, g_head):
    b = qc.shape[0]
    nb = SUBLANES
    hd = M_HEAD_DIM
    rows = lambda w: pl.BlockSpec((nb, w), lambda i: (i, 0))
    per_b = lambda shape: pl.BlockSpec((nb,) + shape, lambda i: (i,) + (0,) * len(shape))
    const = lambda shape: pl.BlockSpec(shape, lambda i: (0,) * len(shape))
    hm, c_new, n_new, m_new = pl.pallas_call(
        _mlstm_step_kernel,
        grid=(b // nb,),
        in_specs=[rows(M_WIDTH), rows(M_WIDTH), rows(M_WIDTH), rows(M_WIDTH), rows(LANES),
                  per_b((M_HEADS, hd, hd)), per_b((M_HEADS, hd)), rows(LANES),
                  const((1, LANES)), const((1, M_WIDTH))],
        out_specs=[rows(M_WIDTH), per_b((M_HEADS, hd, hd)), per_b((M_HEADS, hd)), rows(LANES)],
        out_shape=[jax.ShapeDtypeStruct((b, M_WIDTH), F32),
                   jax.ShapeDtypeStruct((b, M_HEADS, hd, hd), F32),
                   jax.ShapeDtypeStruct((b, M_HEADS, hd), F32),
                   jax.ShapeDtypeStruct((b, LANES), F32)],
        compiler_params=_cparams(("arbitrary",)),
        name="mlstm_sample",
    )(qc, kc, vm, om, gt, c0, n0, jnp.pad(m0, ((0, 0), (0, LANES - M_HEADS))), b_gate_row, g_head)
    return hm.astype(BF16), c_new, n_new, m_new[:, :M_HEADS]


def _sbp_kernel(bias_ref, q_ref, k_ref, v_ref, o_ref, carry_ref, acc_ref):
    blk = CHUNK
    tq = q_ref.shape[0]
    sub = tq // blk
    qi = pl.program_id(2)
    row = lax.broadcasted_iota(jnp.int32, (blk, blk), 0)
    col = lax.broadcasted_iota(jnp.int32, (blk, blk), 1)
    uo = jnp.concatenate([jnp.where(row > col, 1.0, 0.0), jnp.ones((blk, blk), F32)], axis=1).astype(BF16)
    uo2 = jnp.concatenate([uo, uo], axis=0)
    first_head = col < SB_HEAD_DIM
    bias2 = bias_ref[...]
    carry_ref[...] = jnp.zeros_like(carry_ref)
    acc_ref[...] = jnp.zeros_like(acc_ref)

    def head_pair_rows(x):
        zero = jnp.zeros_like(x)
        return jnp.concatenate([jnp.where(first_head, x, zero), jnp.where(first_head, zero, x)], axis=0)

    def tile(j, r0, diag):
        n = tq - r0
        ks = pl.ds(pl.multiple_of(j * blk, blk), blk)
        z = _dot_nt(q_ref[r0:tq, :], head_pair_rows(k_ref[ks, :])) + bias2
        ls, lstay = _log_sigmoid_parts(z)
        if diag:
            q_pos = qi * tq + r0 + lax.broadcasted_iota(jnp.int32, (n, 2 * blk), 0)
            k_pos = j * blk + (lax.broadcasted_iota(jnp.int32, (n, 2 * blk), 1) & (blk - 1))
            valid = k_pos < q_pos
            lstay = jnp.where(valid, lstay, 0.0)
        hi, lo = _split2(lstay)
        ra = _dot(jnp.concatenate([hi[:, :blk], lo[:, :blk]], axis=1), uo2)
        rb = _dot(jnp.concatenate([hi[:, blk:], lo[:, blk:]], axis=1), uo2)
        later = jnp.concatenate([ra[:, :blk], rb[:, :blk]], axis=1) + carry_ref[r0:tq, :]
        a = jnp.exp(ls + later)
        if diag:
            a = jnp.where(valid, a, 0.0)
        acc_ref[r0:tq, :] += _dot(a.astype(BF16), head_pair_rows(v_ref[ks, :]))
        carry_ref[r0:tq, :] += jnp.concatenate([ra[:, blk:], rb[:, blk:]], axis=1)

    for m in reversed(range(sub)):
        tile(qi * sub + m, m * blk, True)

    unroll = 2 if sub % 2 == 0 else 1

    def body(t, c):
        for s in range(unroll):
            tile(qi * sub - 1 - (t * unroll + s), 0, False)
        return c

    lax.fori_loop(0, qi * sub // unroll, body, 0)
    o_ref[...] = acc_ref[...].astype(BF16)


def _sb_prompt(q, k, v, bias):
    b, t, w = q.shape
    tq = min(SBQ, t)
    pairs = w // LANES
    per_pair = LANES // SB_HEAD_DIM
    bias2 = jnp.repeat(bias.reshape(pairs, per_pair), CHUNK, axis=1).reshape(pairs, 1, per_pair * CHUNK)
    return pl.pallas_call(
        _sbp_kernel,
        grid=(b, pairs, t // tq),
        in_specs=[pl.BlockSpec((None, 1, per_pair * CHUNK), lambda i, h, s: (h, 0, 0)),
                  pl.BlockSpec((None, tq, LANES), lambda i, h, s: (i, s, h)),
                  pl.BlockSpec((None, t, LANES), lambda i, h, s: (i, 0, h)),
                  pl.BlockSpec((None, t, LANES), lambda i, h, s: (i, 0, h))],
        out_specs=pl.BlockSpec((None, tq, LANES), lambda i, h, s: (i, s, h)),
        out_shape=jax.ShapeDtypeStruct((b, t, w), BF16),
        scratch_shapes=[pltpu.VMEM((tq, per_pair * CHUNK), F32), pltpu.VMEM((tq, LANES), F32)],
        compiler_params=_cparams(("arbitrary", "arbitrary", "arbitrary")),
        name="sb_prompt",
    )(bias2, q, k, v)


def _sbs_group(first, bias_ref, qb_ref, k_refs, v_refs, carry_ref, z_ref, a_ref, acc_ref):
    G = len(k_refs)
    H = SB_HEADS
    page = k_refs[0].shape[-1]
    for g in range(G):
        for h in range(H):
            z_ref[g * H + h:g * H + h + 1, :] = jnp.sum(k_refs[g][h] * qb_ref[h], axis=0, keepdims=True)
    ls, lstay = _log_sigmoid_parts(z_ref[...] + bias_ref[...])

    row = lax.broadcasted_iota(jnp.int32, (page, page), 0)
    col = lax.broadcasted_iota(jnp.int32, (page, page), 1)
    uo = jnp.concatenate([jnp.where(row > col, 1.0, 0.0), jnp.ones((page, page), F32)], axis=1).astype(BF16)
    hi, lo = _split2(lstay)
    r = _dot(jnp.concatenate([hi, lo], axis=1), jnp.concatenate([uo, uo], axis=0))
    within, tot = r[:, :page], r[:, page:]
    n = G * H
    rr = lax.broadcasted_iota(jnp.int32, (n, n), 0)
    rc = lax.broadcasted_iota(jnp.int32, (n, n), 1)
    shift = H.bit_length() - 1
    later_page = ((rr & (H - 1)) == (rc & (H - 1))) & ((rc >> shift) > (rr >> shift))
    mx = jnp.where(later_page, 1.0, 0.0).astype(BF16)
    thi, tlo = _split2(tot)
    carry = jnp.where(first, 0.0, carry_ref[...])
    a_ref[...] = jnp.exp(ls + within + _dot(mx, thi) + _dot(mx, tlo) + jnp.tile(carry, (G, 1)))
    for g in range(G):
        carry = carry + tot[g * H:(g + 1) * H, :]
    carry_ref[...] = carry

    for h in range(H):
        acc = jnp.where(first, 0.0, acc_ref[h])
        for g in range(G):
            acc = acc + v_refs[g][h] * a_ref[g * H + h:g * H + h + 1, :]
        acc_ref[h] = acc


def _sbs_finish(acc_ref, o_ref):
    page = acc_ref.shape[-1]
    ones = jnp.ones((SUBLANES, page), BF16)
    for h in range(SB_HEADS):
        x = acc_ref[h]
        xh = x.astype(BF16)
        r1 = x - xh.astype(F32)
        xm = r1.astype(BF16)
        xl = (r1 - xm.astype(F32)).astype(BF16)
        s = _dot_nt(ones, xh) + _dot_nt(ones, xm) + _dot_nt(ones, xl)
        o_ref[h:h + 1, :] = s[0:1, :]


def _sbs_kernel(pt_ref, bias_ref, qb_ref, *refs):
    del pt_ref
    G = PAGES_PER_STEP
    k_refs, v_refs = refs[:G], refs[G:2 * G]
    o_ref, carry_ref, z_ref, a_ref, acc_ref = refs[2 * G:]
    j = pl.program_id(1)
    _sbs_group(j == 0, bias_ref, qb_ref, k_refs, v_refs, carry_ref, z_ref, a_ref, acc_ref)

    @pl.when(j == pl.num_programs(1) - 1)
    def _():
        _sbs_finish(acc_ref, o_ref)


def _sb_sample(layer, q_bf, cache_k, cache_v, page_table, bias):
    b = q_bf.shape[0]
    n_pages = page_table.shape[1]
    page = cache_k.shape[2]
    G = PAGES_PER_STEP
    H = SB_HEADS
    groups = n_pages // G
    kt = jnp.transpose(cache_k, (0, 1, 3, 4, 2))
    vt = jnp.transpose(cache_v, (0, 1, 3, 4, 2))
    qb = jnp.broadcast_to(q_bf.astype(F32).reshape(b, H, SB_HEAD_DIM, 1), (b, H, SB_HEAD_DIM, page))
    bias_rows = jnp.broadcast_to(jnp.tile(bias, G).reshape(G * H, 1), (G * H, page))

    def page_spec(g):
        return pl.BlockSpec(
            (None, None, H, SB_HEAD_DIM, page),
            lambda i, j, pt: (layer, pt[i, (groups - 1 - j) * G + g], 0, 0, 0))

    out = pl.pallas_call(
        _sbs_kernel,
        grid_spec=pltpu.PrefetchScalarGridSpec(
            num_scalar_prefetch=1,
            grid=(b, groups),
            in_specs=[pl.BlockSpec((G * H, page), lambda i, j, pt: (0, 0)),
                      pl.BlockSpec((None, H, SB_HEAD_DIM, page), lambda i, j, pt: (i, 0, 0, 0))]
                     + [page_spec(g) for g in range(G)] * 2,
            out_specs=pl.BlockSpec((None, H, SB_HEAD_DIM), lambda i, j, pt: (i, 0, 0)),
            scratch_shapes=[pltpu.VMEM((H, page), F32), pltpu.VMEM((G * H, page), F32),
                            pltpu.VMEM((G * H, page), F32), pltpu.VMEM((H, SB_HEAD_DIM, page), F32)],
        ),
        out_shape=jax.ShapeDtypeStruct((b, H, SB_HEAD_DIM), F32),
        compiler_params=_cparams(("arbitrary", "arbitrary")),
        name="sb_sample",
    )(page_table, bias_rows, qb, *([kt] * G), *([vt] * G))
    return out.reshape(b, SB_WIDTH).astype(BF16)


def _ffn_kernel(x_ref, hm_ref, hs_ref, ga1_ref, sh2_ref, sc2_ref, ga2_ref, g2_ref,
                wo_ref, wg_ref, wu_ref, wd_ref, *rest, final):
    if final:
        gf_ref, shf_ref, scf_ref, o_ref = rest
    else:
        o_ref, = rest
    attn = _dot(hm_ref[...], wo_ref[0:M_WIDTH, :]) + _dot(hs_ref[...], wo_ref[M_WIDTH:, :])
    x1 = x_ref[...] + ga1_ref[...] * attn
    h2 = _rms_mod(x1, g2_ref[...], sc2_ref[...], sh2_ref[...]).astype(BF16)
    gate = _dot(h2, wg_ref[...])
    up = _dot(h2, wu_ref[...])
    act = (gate * jax.nn.sigmoid(gate) * up).astype(BF16)
    x2 = x1 + ga2_ref[...] * _dot(act, wd_ref[...])
    if final:
        x2 = _rms_mod(x2, gf_ref[...], scf_ref[...], shf_ref[...])
    o_ref[...] = x2


def _ffn(x2, hm, hs, mod, g2, wo, wg, wu, wd, group, tm, seq, n_sample, final=None):
    n, d = x2.shape
    mods = [_mod_operand(mod, c, group, tm, seq, n_sample, d) for c in (2, 3, 4, 5)]
    row = lambda w: pl.BlockSpec((tm, w), lambda i: (i, 0))
    vec = pl.BlockSpec((1, d), lambda i: (0, 0))
    resident = lambda w: pl.BlockSpec(w.shape, lambda i: (0, 0), pipeline_mode=pl.Buffered(1))
    operands = [x2, hm, hs] + [a for a, _ in mods] + [g2, wo, wg, wu, wd]
    in_specs = ([row(d), row(M_WIDTH), row(SB_WIDTH)] + [s for _, s in mods]
                + [vec, resident(wo), resident(wg), resident(wu), resident(wd)])
    if final is not None:
        mod_f, g_f = final
        fmods = [_mod_operand(mod_f, c, group, tm, seq, n_sample, d) for c in (0, 1)]
        operands += [g_f] + [a for a, _ in fmods]
        in_specs += [vec] + [s for _, s in fmods]
    return pl.pallas_call(
        functools.partial(_ffn_kernel, final=final is not None),
        grid=(n // tm,),
        in_specs=in_specs,
        out_specs=row(d),
        out_shape=jax.ShapeDtypeStruct((n, d), F32),
        compiler_params=_cparams(("arbitrary",)),
        name="ffn_" + group,
    )(*operands)


def kernel(x_prompt, x_sample, cache_sb_k, cache_sb_v, state_conv, state_mlstm_C, state_mlstm_n,
           state_mlstm_m, page_table, c_prompt, c_sample, w_ada, b_ada, norm1_g, w_in, b_gate, sb_bias,
           conv_w, conv_b, head_norm_g, w_out, norm2_g, w_gate, w_up, w_down, w_ada_f, b_ada_f, norm_f_g):
    bp, seq, d = x_prompt.shape
    bs = x_sample.shape[0]
    depth = w_ada.shape[0]
    n_gate = 2 * M_HEADS
    assert x_sample.shape[1] == 1 and seq % CHUNK == 0
    assert page_table.shape[1] % PAGES_PER_STEP == 0

    c_all = jnp.concatenate([c_sample, c_prompt], axis=0)
    xp = x_prompt.reshape(bp * seq, d)
    xs = x_sample.reshape(bs, d)
    tm_p = min(1024, seq)
    tm_f = min(512, seq)
    assert seq % tm_p == 0 and seq % tm_f == 0

    assert bp % MLSTM_SEQS_PER_STEP == 0 and bs % SUBLANES == 0
    zeros_c = jnp.zeros((bp, M_HEADS, M_HEAD_DIM, M_HEAD_DIM), F32)
    zeros_n = jnp.zeros((bp, M_HEADS, M_HEAD_DIM), F32)
    zeros_m = jnp.zeros((bp, 1, M_HEADS), F32)

    outs = {k: [] for k in ("ks", "vs", "cvp", "cvs", "Cp", "Cs", "np", "ns", "mp", "ms")}
    kt_all = jnp.zeros((depth, bp, SB_WIDTH, seq), F32)
    vt_all = jnp.zeros((depth, bp, SB_WIDTH, seq), F32)
    mod_f = _ada(c_all, w_ada_f[None], b_ada_f[None], 0)
    gf = norm_f_g.reshape(1, d)
    for l in range(depth):
        mod = _ada(c_all, w_ada, b_ada, l)
        wt = _pack_w_in(w_in[l])
        wo, wg, wu, wd = (w_out[l].astype(BF16), w_gate[l].astype(BF16), w_up[l].astype(BF16),
                          w_down[l].astype(BF16))
        g1 = norm1_g[l].reshape(1, d)
        g2 = norm2_g[l].reshape(1, d)
        bg_row = jnp.pad(b_gate[l], (0, LANES - n_gate)).reshape(1, LANES)
        cb = conv_b[l].reshape(1, 2 * M_WIDTH)
        gh = head_norm_g[l].reshape(1, M_WIDTH)
        final = (mod_f, gf) if l == depth - 1 else None

        qc, kc, vm, om, gt, q, kb, vb, kt_all, vt_all, cs = _inproj_prompt(
            xp, mod, g1, wt, conv_w[l], cb, kt_all, vt_all, l, tm_p, seq, bs)
        r3 = lambda a: a.reshape(bp, seq, a.shape[-1])
        hm, c_new, n_new, m_new = _mlstm(r3(qc), r3(kc), r3(vm), r3(om), r3(gt), zeros_c, zeros_n,
                                         zeros_m, bg_row, gh)
        hs = _sb_prompt(r3(q), r3(kb), r3(vb), sb_bias[l])
        xp = _ffn(xp, hm.reshape(bp * seq, M_WIDTH), hs.reshape(bp * seq, SB_WIDTH), mod, g2,
                  wo, wg, wu, wd, "prompt", tm_f, seq, bs, final)
        outs["cvp"].append(cs[:, SUBLANES - (CONV_W - 1):, :])
        outs["Cp"].append(c_new)
        outs["np"].append(n_new)
        outs["mp"].append(m_new.reshape(bp, M_HEADS))

        qk, qc, kc, vm, om, gt, q, kf, vf = _inproj_sample(xs, mod, g1, wt, conv_w[l], cb, state_conv[l])
        hm, c_new, n_new, m_new = _mlstm_step(qc, kc, vm, om, gt, state_mlstm_C[l], state_mlstm_n[l],
                                              state_mlstm_m[l], bg_row, gh)
        hs = _sb_sample(l, q, cache_sb_k, cache_sb_v, page_table, sb_bias[l])
        xs = _ffn(xs, hm, hs, mod, g2, wo, wg, wu, wd, "sample", bs, 1, bs, final)
        outs["ks"].append(kf.reshape(bs, 1, SB_HEADS, SB_HEAD_DIM))
        outs["vs"].append(vf.reshape(bs, 1, SB_HEADS, SB_HEAD_DIM))
        outs["cvs"].append(jnp.concatenate([state_conv[l][:, 1:, :], qk[:, None, :]], axis=1))
        outs["Cs"].append(c_new)
        outs["ns"].append(n_new)
        outs["ms"].append(m_new.reshape(bs, M_HEADS))

    y_prompt = xp.reshape(bp, seq, d)
    y_sample = xs.reshape(bs, 1, d)
    st = lambda k: jnp.stack(outs[k])
    tok_major = lambda a: jnp.transpose(a.reshape(depth, bp, SB_HEADS, SB_HEAD_DIM, seq), (0, 1, 4, 2, 3))
    return (y_prompt, y_sample, tok_major(kt_all), tok_major(vt_all), st("ks"), st("vs"),
            st("cvp"), st("cvs"), st("Cp"), st("Cs"), st("np"), st("ns"), st("mp"), st("ms"))
```

```python
import functools

import jax
import jax.numpy as jnp
from jax import lax
from jax.experimental import pallas as pl
from jax.experimental.pallas import tpu as pltpu

F32 = jnp.float32
BF16 = jnp.bfloat16

RMS_EPS = 1e-6
M_HEADS = 4
M_HEAD_DIM = 128
M_WIDTH = M_HEADS * M_HEAD_DIM
SB_HEADS = 8
SB_HEAD_DIM = 64
SB_WIDTH = SB_HEADS * SB_HEAD_DIM
CONV_W = 4
CHUNK = 128
LANES = 128
SUBLANES = 8
PAGES_PER_STEP = 16
SBQ = 2048
MLSTM_SEQS_PER_STEP = 8
LOG2_E = 1.4426950408889634
NEG_BIG = -1e30
VMEM_LIMIT = 56 * 1024 * 1024


def _cparams(sem):
    return pltpu.CompilerParams(dimension_semantics=sem, vmem_limit_bytes=VMEM_LIMIT)


def _log_sigmoid_parts(z):
    l1p = jnp.log(1.0 + jnp.exp2(jnp.abs(z) * (-LOG2_E)))
    ls = jnp.minimum(z, 0.0) - l1p
    return ls, ls - z


def _split2(x):
    hi = x.astype(BF16)
    lo = (x - hi.astype(F32)).astype(BF16)
    return hi, lo


def _rms_mod(x, g, sc, sh):
    y = x * lax.rsqrt(jnp.mean(x * x, axis=-1, keepdims=True) + RMS_EPS)
    return (y * g) * (1.0 + sc) + sh


def _dot(a, b):
    return jnp.dot(a, b, preferred_element_type=F32)


def _dot_nt(a, b):
    return lax.dot_general(a, b, (((1,), (1,)), ((), ())), preferred_element_type=F32)


def _ada_kernel(c_ref, w_ref, b_ref, o_ref):
    c = c_ref[...]
    a = (c * jax.nn.sigmoid(c)).astype(BF16)
    o_ref[...] = _dot(a, w_ref[...].astype(BF16)) + b_ref[...]


def _ada(c_all, w, b, layer):
    m, d = c_all.shape
    n = w.shape[2]
    tn = 1024
    return pl.pallas_call(
        _ada_kernel,
        grid=(n // tn,),
        in_specs=[pl.BlockSpec((m, d), lambda j: (0, 0)),
                  pl.BlockSpec((None, d, tn), lambda j: (layer, 0, j)),
                  pl.BlockSpec((None, 1, tn), lambda j: (layer, 0, j))],
        out_specs=pl.BlockSpec((m, tn), lambda j: (0, j)),
        out_shape=jax.ShapeDtypeStruct((m, n), F32),
        compiler_params=_cparams(("arbitrary",)),
        name="ada_mod",
    )(c_all, w, b.reshape(b.shape[0], 1, n))


def _mod_operand(mod, chunk, group, tm, seq, n_sample, d):
    if group == "sample":
        return mod, pl.BlockSpec((n_sample, d), lambda i, *_: (0, chunk))
    mod3 = mod.reshape(mod.shape[0], 1, mod.shape[1])
    return mod3, pl.BlockSpec((None, 1, d), lambda i, *_: (n_sample + (i * tm) // seq, 0, chunk))


_SEG_QK = (0, 2 * M_WIDTH)
_SEG_VM = (_SEG_QK[1], _SEG_QK[1] + M_WIDTH)
_SEG_OM = (_SEG_VM[1], _SEG_VM[1] + M_WIDTH)
_SEG_Q = (_SEG_OM[1], _SEG_OM[1] + SB_WIDTH)
_SEG_K = (_SEG_Q[1], _SEG_Q[1] + SB_WIDTH)
_SEG_V = (_SEG_K[1], _SEG_K[1] + SB_WIDTH)
_SEG_GATES = (_SEG_V[1], _SEG_V[1] + LANES)


def _pack_w_in(w_in_l):
    n_gate = 2 * M_HEADS
    g0 = 4 * M_WIDTH
    wt = w_in_l.T
    return jnp.concatenate(
        [wt[:g0], wt[g0 + n_gate:], jnp.pad(wt[g0:g0 + n_gate], ((0, LANES - n_gate), (0, 0)))],
        axis=0).astype(BF16)


def _conv_silu_split(acc, qc_ref, kc_ref):
    qkc = acc * jax.nn.sigmoid(acc)
    qc_ref[...] = qkc[:, :M_WIDTH]
    kc_ref[...] = qkc[:, M_WIDTH:] * (M_HEAD_DIM ** -0.5)


def _inproj_prompt_kernel(x_ref, g_ref, sc_ref, sh_ref, wt_ref, cw_ref, cb_ref, kt_all_ref, vt_all_ref,
                          qc_ref, kc_ref, vm_ref, om_ref, gt_ref, q_ref, kb_ref, vb_ref,
                          kt_ref, vt_ref, cs_ref, ubuf, *, tiles_per_seq):
    del kt_all_ref, vt_all_ref
    tm = x_ref.shape[0]
    h = _rms_mod(x_ref[...], g_ref[...], sc_ref[...], sh_ref[...]).astype(BF16)
    seg = lambda s: _dot_nt(h, wt_ref[s[0]:s[1], :])

    @pl.when(pl.program_id(0) % tiles_per_seq == 0)
    def _():
        ubuf[0:SUBLANES, :] = jnp.zeros((SUBLANES, ubuf.shape[1]), F32)

    def conv_silu(c0, c1, dst_ref, d0, scale):
        cols = slice(c0, c1)
        u = _dot_nt(h, wt_ref[c0:c1, :])
        ubuf[SUBLANES:SUBLANES + tm, cols] = u
        acc = cb_ref[:, cols] + ubuf[5:5 + tm, cols] * cw_ref[0:1, cols]
        acc = acc + ubuf[6:6 + tm, cols] * cw_ref[1:2, cols]
        acc = acc + ubuf[7:7 + tm, cols] * cw_ref[2:3, cols]
        acc = acc + u * cw_ref[3:4, cols]
        last = ubuf[tm:tm + SUBLANES, cols]
        ubuf[0:SUBLANES, cols] = last
        cs_ref[:, cols] = last
        y = acc * jax.nn.sigmoid(acc)
        if scale is not None:
            y = y * scale
        dst_ref[:, d0:d0 + (c1 - c0)] = y.astype(BF16)

    half = M_WIDTH // 2
    conv_silu(0, half, qc_ref, 0, None)
    vm_ref[...] = seg(_SEG_VM).astype(BF16)
    conv_silu(half, M_WIDTH, qc_ref, half, None)
    om_ref[...] = seg(_SEG_OM)
    conv_silu(M_WIDTH, M_WIDTH + half, kc_ref, 0, M_HEAD_DIM ** -0.5)
    q_ref[...] = (seg(_SEG_Q) * (SB_HEAD_DIM ** -0.5)).astype(BF16)
    conv_silu(M_WIDTH + half, 2 * M_WIDTH, kc_ref, half, M_HEAD_DIM ** -0.5)
    gt_ref[...] = seg(_SEG_GATES)
    k = seg(_SEG_K)
    kb_ref[...] = k.astype(BF16)
    kt_ref[...] = k.T
    v = seg(_SEG_V)
    vb_ref[...] = v.astype(BF16)
    vt_ref[...] = v.T


def _inproj_prompt(x2, mod, g1, wt, conv_w, conv_b, kt_all, vt_all, layer, tm, seq, n_sample):
    n, d = x2.shape
    b = n // seq
    tps = seq // tm
    sh_a, sh_s = _mod_operand(mod, 0, "prompt", tm, seq, n_sample, d)
    sc_a, sc_s = _mod_operand(mod, 1, "prompt", tm, seq, n_sample, d)
    row = lambda w: pl.BlockSpec((tm, w), lambda i: (i, 0))
    const = lambda a: pl.BlockSpec(a.shape, lambda i: (0,) * a.ndim)
    tr = pl.BlockSpec((None, None, SB_WIDTH, tm), lambda i: (layer, i // tps, 0, i % tps))
    hbm = pl.BlockSpec(memory_space=pl.ANY)
    rows = [(M_WIDTH, BF16), (M_WIDTH, BF16), (M_WIDTH, BF16), (M_WIDTH, F32), (LANES, F32),
            (SB_WIDTH, BF16), (SB_WIDTH, BF16), (SB_WIDTH, BF16)]
    operands = (x2, g1, sc_a, sh_a, wt, conv_w, conv_b, kt_all, vt_all)
    return pl.pallas_call(
        functools.partial(_inproj_prompt_kernel, tiles_per_seq=tps),
        grid=(n // tm,),
        in_specs=[row(d), const(g1), sc_s, sh_s,
                  pl.BlockSpec(wt.shape, lambda i: (0, 0), pipeline_mode=pl.Buffered(1)),
                  const(conv_w), const(conv_b), hbm, hbm],
        out_specs=[row(w) for w, _ in rows] + [tr, tr,
                   pl.BlockSpec((None, SUBLANES, 2 * M_WIDTH), lambda i: (i // tps, 0, 0))],
        out_shape=[jax.ShapeDtypeStruct((n, w), dt) for w, dt in rows]
                  + [jax.ShapeDtypeStruct(kt_all.shape, F32), jax.ShapeDtypeStruct(vt_all.shape, F32)]
                  + [jax.ShapeDtypeStruct((b, SUBLANES, 2 * M_WIDTH), F32)],
        input_output_aliases={len(operands) - 2: len(rows), len(operands) - 1: len(rows) + 1},
        scratch_shapes=[pltpu.VMEM((SUBLANES + tm, 2 * M_WIDTH), F32)],
        compiler_params=_cparams(("arbitrary",)),
        name="in_proj_prompt",
    )(*operands)


def _inproj_sample_kernel(x_ref, g_ref, sc_ref, sh_ref, wt_ref, cw_ref, cb_ref, s0_ref, s1_ref, s2_ref,
                          qk_ref, qc_ref, kc_ref, vm_ref, om_ref, gt_ref, q_ref, kf_ref, vf_ref):
    h = _rms_mod(x_ref[...], g_ref[...], sc_ref[...], sh_ref[...]).astype(BF16)
    seg = lambda s: _dot_nt(h, wt_ref[s[0]:s[1], :])
    u = seg(_SEG_QK)
    qk_ref[...] = u
    cw = cw_ref[...]
    acc = cb_ref[...] + s0_ref[...] * cw[0:1, :]
    acc = acc + s1_ref[...] * cw[1:2, :]
    acc = acc + s2_ref[...] * cw[2:3, :]
    acc = acc + u * cw[3:4, :]
    _conv_silu_split(acc, qc_ref, kc_ref)
    vm_ref[...] = seg(_SEG_VM)
    om_ref[...] = seg(_SEG_OM)
    gt_ref[...] = seg(_SEG_GATES)
    q_ref[...] = (seg(_SEG_Q) * (SB_HEAD_DIM ** -0.5)).astype(BF16)
    kf_ref[...] = seg(_SEG_K)
    vf_ref[...] = seg(_SEG_V)


def _inproj_sample(x2, mod, g1, wt, conv_w, conv_b, conv_state):
    n, d = x2.shape
    sh_a, sh_s = _mod_operand(mod, 0, "sample", n, 1, n, d)
    sc_a, sc_s = _mod_operand(mod, 1, "sample", n, 1, n, d)
    full = lambda a: pl.BlockSpec(a.shape, lambda i: (0,) * a.ndim)
    states = [conv_state[:, r, :] for r in range(CONV_W - 1)]
    outs = [(2 * M_WIDTH, F32), (M_WIDTH, F32), (M_WIDTH, F32), (M_WIDTH, F32), (M_WIDTH, F32),
            (LANES, F32), (SB_WIDTH, BF16), (SB_WIDTH, F32), (SB_WIDTH, F32)]
    return pl.pallas_call(
        _inproj_sample_kernel,
        grid=(1,),
        in_specs=[full(x2), full(g1), sc_s, sh_s, full(wt), full(conv_w), full(conv_b)]
                 + [full(s) for s in states],
        out_specs=[pl.BlockSpec((n, w), lambda i: (0, 0)) for w, _ in outs],
        out_shape=[jax.ShapeDtypeStruct((n, w), dt) for w, dt in outs],
        compiler_params=_cparams(("arbitrary",)),
        name="in_proj_sample",
    )(x2, g1, sc_a, sh_a, wt, conv_w, conv_b, *states)


def _mlstm_kernel(q_ref, k_ref, v_ref, o_ref, gt_ref, c0_ref, n0_ref, m0_ref, bg_ref, gh_ref,
                  hm_ref, c_ref, n_ref, m_ref):
    L = CHUNK
    nb = q_ref.shape[0]
    ci = pl.program_id(1)

    @pl.when(ci == 0)
    def _():
        c_ref[...] = c0_ref[...]
        n_ref[...] = n0_ref[...]
        m_ref[...] = m0_ref[...]

    row = lax.broadcasted_iota(jnp.int32, (L, L), 0)
    col = lax.broadcasted_iota(jnp.int32, (L, L), 1)
    tri = row >= col
    tril = jnp.where(tri, 1.0, 0.0).astype(BF16)
    lane = lax.broadcasted_iota(jnp.int32, (L, LANES), 1)
    head_lane = lax.broadcasted_iota(jnp.int32, (1, M_HEADS), 1)
    src = lax.broadcasted_iota(jnp.int32, (3 * LANES, 2 * LANES), 0) & (LANES - 1)
    dst = lax.broadcasted_iota(jnp.int32, (3 * LANES, 2 * LANES), 1)
    sel = [jnp.where(src == jnp.where(dst < LANES, M_HEADS + h, h), 1.0, 0.0).astype(BF16)
           for h in range(M_HEADS)]
    n_in = [n_ref[bi] for bi in range(nb)]
    m_in = [m_ref[bi] for bi in range(nb)]
    stores = []

    items = [(bi, h) for bi in range(nb) for h in range(M_HEADS)]
    gates = []
    for bi in range(nb):
        gf = gt_ref[bi] + bg_ref[...]
        lsg, _ = _log_sigmoid_parts(gf)
        li_all = gf
        hi = lsg.astype(BF16)
        r1 = lsg - hi.astype(F32)
        mid = r1.astype(BF16)
        lo = (r1 - mid.astype(F32)).astype(BF16)
        bcum = _dot(tril, hi) + _dot(tril, mid) + _dot(tril, lo)
        gmat = jnp.where(lane < M_HEADS, li_all, bcum)
        gmat_t = gmat.T
        ghi = gmat.astype(BF16)
        g1 = gmat - ghi.astype(F32)
        gmid = g1.astype(BF16)
        g3 = jnp.concatenate([ghi, gmid, (g1 - gmid.astype(F32)).astype(BF16)], axis=1)
        gates.append((gmat_t, g3))

    sl = [slice(h * M_HEAD_DIM, (h + 1) * M_HEAD_DIM) for h in range(M_HEADS)]
    qs = [q_ref[bi, :, sl[h]] for bi, h in items]
    ks = [k_ref[bi, :, sl[h]] for bi, h in items]
    vs = [v_ref[bi, :, sl[h]] for bi, h in items]
    cs = [c_ref[bi, h] for bi, h in items]
    ns = [n_in[bi][h:h + 1, :] for bi, h in items]
    li_row = [gates[bi][0][h:h + 1, :] for bi, h in items]
    b_row = [gates[bi][0][M_HEADS + h:M_HEADS + h + 1, :] for bi, h in items]
    m_prev = [jnp.sum(jnp.where(head_lane == h, m_in[bi], 0.0), axis=-1, keepdims=True) for bi, h in items]
    idx = range(len(items))

    rep = [_dot(gates[bi][1], sel[h]) for bi, h in items]
    b_col = [r[:, :LANES] for r in rep]
    li_col = [r[:, LANES:] for r in rep]
    s_qk = [_dot_nt(qs[i], ks[i]) for i in idx]
    q_c = [_dot(qs[i], cs[i].astype(BF16)) for i in idx]
    q_n = []
    for i in idx:
        nhi, nlo = _split2(jnp.broadcast_to(ns[i], (M_HEAD_DIM, M_HEAD_DIM)))
        q_n.append(_dot_nt(qs[i], nhi) + _dot_nt(qs[i], nlo))

    dmat = [jnp.where(tri, b_col[i] - b_row[i] + li_row[i], NEG_BIG) for i in idx]
    row_max = [jnp.max(dmat[i], axis=-1, keepdims=True) for i in idx]
    m_inter = [b_col[i] + m_prev[i] for i in idx]
    m_t = [jnp.maximum(m_inter[i], row_max[i]) for i in idx]
    w_inter = [jnp.exp(m_inter[i] - m_t[i]) for i in idx]
    w_intra = [jnp.exp(dmat[i] - m_t[i]) * s_qk[i] for i in idx]
    w_sum = [jnp.sum(w_intra[i], axis=-1, keepdims=True) for i in idx]
    w_v = [_dot(w_intra[i].astype(BF16), vs[i]) for i in idx]
    hh = []
    for i in idx:
        num = w_inter[i] * q_c[i] + w_v[i]
        den = w_inter[i] * q_n[i] + w_sum[i]
        hh.append(num / jnp.maximum(jnp.abs(den), jnp.exp(-m_t[i])))
    h_ms = [jnp.mean(hh[i] * hh[i], axis=-1, keepdims=True) for i in idx]
    for i, (bi, h) in enumerate(items):
        hn = hh[i] * lax.rsqrt(h_ms[i] + RMS_EPS) * gh_ref[:, sl[h]]
        stores.append((hm_ref, (bi, slice(None), sl[h]),
                       (jax.nn.sigmoid(o_ref[bi, :, sl[h]]) * hn).astype(BF16)))

    b_last = [b_col[i][L - 1:L, :] for i in idx]
    g_max = [jnp.max(b_last[i] - b_row[i] + li_row[i], axis=-1, keepdims=True) for i in idx]
    m_new = [jnp.maximum(b_last[i] + m_prev[i], g_max[i]) for i in idx]
    kg = [ks[i].astype(F32) * jnp.exp(b_last[i] - b_col[i] + li_col[i] - m_new[i]) for i in idx]
    decay = [jnp.exp(b_last[i] + m_prev[i] - m_new[i]) for i in idx]
    kg_t = [kg[i].T.astype(BF16) for i in idx]
    for i, (bi, h) in enumerate(items):
        stores.append((c_ref, (bi, h), decay[i] * cs[i] + _dot(kg_t[i], vs[i])))
    for bi in range(nb):
        mine = [i for i in idx if items[i][0] == bi]
        n_rows = [decay[i] * ns[i] + jnp.sum(kg[i], axis=0, keepdims=True) for i in mine]
        m_out = m_in[bi]
        for i in mine:
            m_out = jnp.where(head_lane == items[i][1], m_new[i][:, :M_HEADS], m_out)
        stores.append((n_ref, (bi,), jnp.concatenate(n_rows, axis=0)))
        stores.append((m_ref, (bi,), m_out))
    for ref, where, val in stores:
        ref[where] = val


def _mlstm(qc, kc, vm, om, gt, c0, n0, m0, b_gate_row, g_head):
    b, t, _ = qc.shape
    nc = t // CHUNK
    nb = MLSTM_SEQS_PER_STEP
    tok = lambda w: pl.BlockSpec((nb, CHUNK, w), lambda i, c: (i, c, 0))
    per_b = lambda shape: pl.BlockSpec((nb,) + shape, lambda i, c: (i,) + (0,) * len(shape))
    const = lambda shape: pl.BlockSpec(shape, lambda i, c: (0,) * len(shape))
    hd = M_HEAD_DIM
    return pl.pallas_call(
        _mlstm_kernel,
        grid=(b // nb, nc),
        in_specs=[tok(M_WIDTH), tok(M_WIDTH), tok(M_WIDTH), tok(M_WIDTH), tok(LANES),
                  per_b((M_HEADS, hd, hd)), per_b((M_HEADS, hd)), per_b((1, M_HEADS)),
                  const((1, LANES)), const((1, M_WIDTH))],
        out_specs=[tok(M_WIDTH), per_b((M_HEADS, hd, hd)), per_b((M_HEADS, hd)), per_b((1, M_HEADS))],
        out_shape=[jax.ShapeDtypeStruct((b, t, M_WIDTH), BF16),
                   jax.ShapeDtypeStruct((b, M_HEADS, hd, hd), F32),
                   jax.ShapeDtypeStruct((b, M_HEADS, hd), F32),
                   jax.ShapeDtypeStruct((b, 1, M_HEADS), F32)],
        compiler_params=_cparams(("arbitrary", "arbitrary")),
        name="mlstm_prompt",
    )(qc, kc, vm, om, gt, c0, n0, m0, b_gate_row, g_head)


def _mlstm_step_kernel(q_ref, k_ref, v_ref, o_ref, gt_ref, c0_ref, n0_ref, m0_ref, bg_ref, gh_ref,
                       hm_ref, c_ref, n_ref, m_ref):
    nb = q_ref.shape[0]
    hd = M_HEAD_DIM
    gf = gt_ref[...] + bg_ref[...]
    lsg, _ = _log_sigmoid_parts(gf)
    lf = pltpu.roll(lsg, LANES - M_HEADS, axis=1)
    m_old = m0_ref[...]
    m_new = jnp.maximum(lf + m_old, gf)
    decay_all = jnp.exp(lf + m_old - m_new)
    gain_all = jnp.exp(gf - m_new)
    floor_all = jnp.exp(-m_new)
    m_ref[...] = m_new

    lane = lax.broadcasted_iota(jnp.int32, (1, LANES), 1)

    def pick(x, bi, h):
        return jnp.sum(jnp.where(lane == h, x[bi:bi + 1, :], 0.0), axis=-1, keepdims=True)

    items = [(bi, h) for bi in range(nb) for h in range(M_HEADS)]
    sl = [slice(h * hd, (h + 1) * hd) for h in range(M_HEADS)]
    idx = range(len(items))
    decay = [pick(decay_all, bi, h) for bi, h in items]
    gain = [pick(gain_all, bi, h) for bi, h in items]
    floor = [pick(floor_all, bi, h) for bi, h in items]
    qs = [q_ref[bi:bi + 1, sl[h]] for bi, h in items]
    vs = [v_ref[bi:bi + 1, sl[h]] for bi, h in items]
    kg = [k_ref[bi:bi + 1, sl[h]] * gain[i] for i, (bi, h) in enumerate(items)]
    kg_col = [jnp.broadcast_to(kg[i], (hd, hd)).T for i in idx]
    c_new = [decay[i] * c0_ref[bi, h] + kg_col[i] * vs[i] for i, (bi, h) in enumerate(items)]
    n_new = [decay[i] * n0_ref[bi, h:h + 1, :] + kg[i] for i, (bi, h) in enumerate(items)]
    num = [_dot(jnp.broadcast_to(qs[i], (SUBLANES, hd)).astype(BF16), c_new[i].astype(BF16))[0:1, :]
           for i in idx]
    den = [jnp.sum(qs[i] * n_new[i], axis=-1, keepdims=True) for i in idx]
    hh = [num[i] / jnp.maximum(jnp.abs(den[i]), floor[i]) for i in idx]
    h_ms = [jnp.mean(hh[i] * hh[i], axis=-1, keepdims=True) for i in idx]
    for i, (bi, h) in enumerate(items):
        hn = hh[i] * lax.rsqrt(h_ms[i] + RMS_EPS) * gh_ref[:, sl[h]]
        hm_ref[bi:bi + 1, sl[h]] = jax.nn.sigmoid(o_ref[bi:bi + 1, sl[h]]) * hn
        c_ref[bi, h] = c_new[i]
        n_ref[bi, h:h + 1, :] = n_new[i]


def _mlstm_step(qc, kc, vm, om, gt, c0, n0, m0, b_gate_row, g_head):
    b = qc.shape[0]
    nb = SUBLANES
    hd = M_HEAD_DIM
    rows = lambda w: pl.BlockSpec((nb, w), lambda i: (i, 0))
    per_b = lambda shape: pl.BlockSpec((nb,) + shape, lambda i: (i,) + (0,) * len(shape))
    const = lambda shape: pl.BlockSpec(shape, lambda i: (0,) * len(shape))
    hm, c_new, n_new, m_new = pl.pallas_call(
        _mlstm_step_kernel,
        grid=(b // nb,),
        in_specs=[rows(M_WIDTH), rows(M_WIDTH), rows(M_WIDTH), rows(M_WIDTH), rows(LANES),
                  per_b((M_HEADS, hd, hd)), per_b((M_HEADS, hd)), rows(LANES),
                  const((1, LANES)), const((1, M_WIDTH))],
        out_specs=[rows(M_WIDTH), per_b((M_HEADS, hd, hd)), per_b((M_HEADS, hd)), rows(LANES)],
        out_shape=[jax.ShapeDtypeStruct((b, M_WIDTH), F32),
                   jax.ShapeDtypeStruct((b, M_HEADS, hd, hd), F32),
                   jax.ShapeDtypeStruct((b, M_HEADS, hd), F32),
                   jax.ShapeDtypeStruct((b, LANES), F32)],
        compiler_params=_cparams(("arbitrary",)),
        name="mlstm_sample",
    )(qc, kc, vm, om, gt, c0, n0, jnp.pad(m0, ((0, 0), (0, LANES - M_HEADS))), b_gate_row, g_head)
    return hm.astype(BF16), c_new, n_new, m_new[:, :M_HEADS]


def _sbp_kernel(bias_ref, q_ref, k_ref, v_ref, o_ref, carry_ref, acc_ref):
    blk = CHUNK
    tq = q_ref.shape[0]
    sub = tq // blk
    qi = pl.program_id(2)
    row = lax.broadcasted_iota(jnp.int32, (blk, blk), 0)
    col = lax.broadcasted_iota(jnp.int32, (blk, blk), 1)
    uo = jnp.concatenate([jnp.where(row > col, 1.0, 0.0), jnp.ones((blk, blk), F32)], axis=1).astype(BF16)
    uo2 = jnp.concatenate([uo, uo], axis=0)
    first_head = col < SB_HEAD_DIM
    bias2 = bias_ref[...]
    carry_ref[...] = jnp.zeros_like(carry_ref)
    acc_ref[...] = jnp.zeros_like(acc_ref)

    def head_pair_rows(x):
        zero = jnp.zeros_like(x)
        return jnp.concatenate([jnp.where(first_head, x, zero), jnp.where(first_head, zero, x)], axis=0)

    strict = jnp.concatenate([col < row, col < row], axis=1)

    def top_rows(x, fn):
        if x.shape[0] == blk:
            return fn(x)
        return jnp.concatenate([fn(x[:blk]), x[blk:]], axis=0)

    def tile(j, r0, diag):
        ks = pl.ds(pl.multiple_of(j * blk, blk), blk)
        z = _dot_nt(q_ref[r0:tq, :], head_pair_rows(k_ref[ks, :])) + bias2
        ls, lstay = _log_sigmoid_parts(z)
        if diag:
            lstay = top_rows(lstay, lambda x: jnp.where(strict, x, 0.0))
        hi, lo = _split2(lstay)
        ra = _dot(jnp.concatenate([hi[:, :blk], lo[:, :blk]], axis=1), uo2)
        rb = _dot(jnp.concatenate([hi[:, blk:], lo[:, blk:]], axis=1), uo2)
        later = jnp.concatenate([ra[:, :blk], rb[:, :blk]], axis=1) + carry_ref[r0:tq, :]
        a = jnp.exp(ls + later)
        if diag:
            a = top_rows(a, lambda x: jnp.where(strict, x, 0.0))
        acc_ref[r0:tq, :] += _dot(a.astype(BF16), head_pair_rows(v_ref[ks, :]))
        carry_ref[r0:tq, :] += jnp.concatenate([ra[:, blk:], rb[:, blk:]], axis=1)

    for m in reversed(range(sub)):
        tile(qi * sub + m, m * blk, True)

    unroll = 2 if sub % 2 == 0 else 1

    def body(t, c):
        for s in range(unroll):
            tile(qi * sub - 1 - (t * unroll + s), 0, False)
        return c

    lax.fori_loop(0, qi * sub // unroll, body, 0)
    o_ref[...] = acc_ref[...].astype(BF16)


def _sb_prompt(q, k, v, bias):
    b, t, w = q.shape
    tq = min(SBQ, t)
    pairs = w // LANES
    per_pair = LANES // SB_HEAD_DIM
    bias2 = jnp.repeat(bias.reshape(pairs, per_pair), CHUNK, axis=1).reshape(pairs, 1, per_pair * CHUNK)
    return pl.pallas_call(
        _sbp_kernel,
        grid=(b, pairs, t // tq),
        in_specs=[pl.BlockSpec((None, 1, per_pair * CHUNK), lambda i, h, s: (h, 0, 0)),
                  pl.BlockSpec((None, tq, LANES), lambda i, h, s: (i, s, h)),
                  pl.BlockSpec((None, t, LANES), lambda i, h, s: (i, 0, h)),
                  pl.BlockSpec((None, t, LANES), lambda i, h, s: (i, 0, h))],
        out_specs=pl.BlockSpec((None, tq, LANES), lambda i, h, s: (i, s, h)),
        out_shape=jax.ShapeDtypeStruct((b, t, w), BF16),
        scratch_shapes=[pltpu.VMEM((tq, per_pair * CHUNK), F32), pltpu.VMEM((tq, LANES), F32)],
        compiler_params=_cparams(("arbitrary", "arbitrary", "arbitrary")),
        name="sb_prompt",
    )(bias2, q, k, v)


def _sbs_group(first, bias_ref, qb_ref, k_refs, v_refs, carry_ref, z_ref, a_ref, acc_ref):
    G = len(k_refs)
    H = SB_HEADS
    page = k_refs[0].shape[-1]
    for g in range(G):
        for h in range(H):
            z_ref[g * H + h:g * H + h + 1, :] = jnp.sum(k_refs[g][h] * qb_ref[h], axis=0, keepdims=True)
    ls, lstay = _log_sigmoid_parts(z_ref[...] + bias_ref[...])

    row = lax.broadcasted_iota(jnp.int32, (page, page), 0)
    col = lax.broadcasted_iota(jnp.int32, (page, page), 1)
    uo = jnp.concatenate([jnp.where(row > col, 1.0, 0.0), jnp.ones((page, page), F32)], axis=1).astype(BF16)
    hi, lo = _split2(lstay)
    r = _dot(jnp.concatenate([hi, lo], axis=1), jnp.concatenate([uo, uo], axis=0))
    within, tot = r[:, :page], r[:, page:]
    n = G * H
    rr = lax.broadcasted_iota(jnp.int32, (n, n), 0)
    rc = lax.broadcasted_iota(jnp.int32, (n, n), 1)
    shift = H.bit_length() - 1
    later_page = ((rr & (H - 1)) == (rc & (H - 1))) & ((rc >> shift) > (rr >> shift))
    mx = jnp.where(later_page, 1.0, 0.0).astype(BF16)
    thi, tlo = _split2(tot)
    carry = jnp.where(first, 0.0, carry_ref[...])
    a_ref[...] = jnp.exp(ls + within + _dot(mx, thi) + _dot(mx, tlo) + jnp.tile(carry, (G, 1)))
    for g in range(G):
        carry = carry + tot[g * H:(g + 1) * H, :]
    carry_ref[...] = carry

    for h in range(H):
        acc = jnp.where(first, 0.0, acc_ref[h])
        for g in range(G):
            acc = acc + v_refs[g][h] * a_ref[g * H + h:g * H + h + 1, :]
        acc_ref[h] = acc


def _sbs_finish(acc_ref, o_ref):
    page = acc_ref.shape[-1]
    ones = jnp.ones((SUBLANES, page), BF16)
    for h in range(SB_HEADS):
        x = acc_ref[h]
        xh = x.astype(BF16)
        r1 = x - xh.astype(F32)
        xm = r1.astype(BF16)
        xl = (r1 - xm.astype(F32)).astype(BF16)
        s = _dot_nt(ones, xh) + _dot_nt(ones, xm) + _dot_nt(ones, xl)
        o_ref[h:h + 1, :] = s[0:1, :]


def _sbs_kernel(pt_ref, bias_ref, qb_ref, *refs):
    del pt_ref
    G = PAGES_PER_STEP
    k_refs, v_refs = refs[:G], refs[G:2 * G]
    o_ref, carry_ref, z_ref, a_ref, acc_ref = refs[2 * G:]
    j = pl.program_id(1)
    _sbs_group(j == 0, bias_ref, qb_ref, k_refs, v_refs, carry_ref, z_ref, a_ref, acc_ref)

    @pl.when(j == pl.num_programs(1) - 1)
    def _():
        _sbs_finish(acc_ref, o_ref)


def _sb_sample(layer, q_bf, cache_k, cache_v, page_table, bias):
    b = q_bf.shape[0]
    n_pages = page_table.shape[1]
    page = cache_k.shape[2]
    G = PAGES_PER_STEP
    H = SB_HEADS
    groups = n_pages // G
    kt = jnp.transpose(cache_k, (0, 1, 3, 4, 2))
    vt = jnp.transpose(cache_v, (0, 1, 3, 4, 2))
    qb = jnp.broadcast_to(q_bf.astype(F32).reshape(b, H, SB_HEAD_DIM, 1), (b, H, SB_HEAD_DIM, page))
    bias_rows = jnp.broadcast_to(jnp.tile(bias, G).reshape(G * H, 1), (G * H, page))

    def page_spec(g):
        return pl.BlockSpec(
            (None, None, H, SB_HEAD_DIM, page),
            lambda i, j, pt: (layer, pt[i, (groups - 1 - j) * G + g], 0, 0, 0))

    out = pl.pallas_call(
        _sbs_kernel,
        grid_spec=pltpu.PrefetchScalarGridSpec(
            num_scalar_prefetch=1,
            grid=(b, groups),
            in_specs=[pl.BlockSpec((G * H, page), lambda i, j, pt: (0, 0)),
                      pl.BlockSpec((None, H, SB_HEAD_DIM, page), lambda i, j, pt: (i, 0, 0, 0))]
                     + [page_spec(g) for g in range(G)] * 2,
            out_specs=pl.BlockSpec((None, H, SB_HEAD_DIM), lambda i, j, pt: (i, 0, 0)),
            scratch_shapes=[pltpu.VMEM((H, page), F32), pltpu.VMEM((G * H, page), F32),
                            pltpu.VMEM((G * H, page), F32), pltpu.VMEM((H, SB_HEAD_DIM, page), F32)],
        ),
        out_shape=jax.ShapeDtypeStruct((b, H, SB_HEAD_DIM), F32),
        compiler_params=_cparams(("arbitrary", "arbitrary")),
        name="sb_sample",
    )(page_table, bias_rows, qb, *([kt] * G), *([vt] * G))
    return out.reshape(b, SB_WIDTH).astype(BF16)


def _ffn_kernel(x_ref, hm_ref, hs_ref, ga1_ref, sh2_ref, sc2_ref, ga2_ref, g2_ref,
                wo_ref, wg_ref, wu_ref, wd_ref, *rest, final):
    if final:
        gf_ref, shf_ref, scf_ref, o_ref = rest
    else:
        o_ref, = rest
    attn = _dot(hm_ref[...], wo_ref[0:M_WIDTH, :]) + _dot(hs_ref[...], wo_ref[M_WIDTH:, :])
    x1 = x_ref[...] + ga1_ref[...] * attn
    h2 = _rms_mod(x1, g2_ref[...], sc2_ref[...], sh2_ref[...]).astype(BF16)
    gate = _dot(h2, wg_ref[...])
    up = _dot(h2, wu_ref[...])
    act = (gate * jax.nn.sigmoid(gate) * up).astype(BF16)
    x2 = x1 + ga2_ref[...] * _dot(act, wd_ref[...])
    if final:
        x2 = _rms_mod(x2, gf_ref[...], scf_ref[...], shf_ref[...])
    o_ref[...] = x2


def _ffn(x2, hm, hs, mod, g2, wo, wg, wu, wd, group, tm, seq, n_sample, final=None):
    n, d = x2.shape
    mods = [_mod_operand(mod, c, group, tm, seq, n_sample, d) for c in (2, 3, 4, 5)]
    row = lambda w: pl.BlockSpec((tm, w), lambda i: (i, 0))
    vec = pl.BlockSpec((1, d), lambda i: (0, 0))
    resident = lambda w: pl.BlockSpec(w.shape, lambda i: (0, 0), pipeline_mode=pl.Buffered(1))
    operands = [x2, hm, hs] + [a for a, _ in mods] + [g2, wo, wg, wu, wd]
    in_specs = ([row(d), row(M_WIDTH), row(SB_WIDTH)] + [s for _, s in mods]
                + [vec, resident(wo), resident(wg), resident(wu), resident(wd)])
    if final is not None:
        mod_f, g_f = final
        fmods = [_mod_operand(mod_f, c, group, tm, seq, n_sample, d) for c in (0, 1)]
        operands += [g_f] + [a for a, _ in fmods]
        in_specs += [vec] + [s for _, s in fmods]
    return pl.pallas_call(
        functools.partial(_ffn_kernel, final=final is not None),
        grid=(n // tm,),
        in_specs=in_specs,
        out_specs=row(d),
        out_shape=jax.ShapeDtypeStruct((n, d), F32),
        compiler_params=_cparams(("arbitrary",)),
        name="ffn_" + group,
    )(*operands)


def kernel(x_prompt, x_sample, cache_sb_k, cache_sb_v, state_conv, state_mlstm_C, state_mlstm_n,
           state_mlstm_m, page_table, c_prompt, c_sample, w_ada, b_ada, norm1_g, w_in, b_gate, sb_bias,
           conv_w, conv_b, head_norm_g, w_out, norm2_g, w_gate, w_up, w_down, w_ada_f, b_ada_f, norm_f_g):
    bp, seq, d = x_prompt.shape
    bs = x_sample.shape[0]
    depth = w_ada.shape[0]
    n_gate = 2 * M_HEADS
    assert x_sample.shape[1] == 1 and seq % CHUNK == 0
    assert page_table.shape[1] % PAGES_PER_STEP == 0

    c_all = jnp.concatenate([c_sample, c_prompt], axis=0)
    xp = x_prompt.reshape(bp * seq, d)
    xs = x_sample.reshape(bs, d)
    tm_p = min(1024, seq)
    tm_f = min(512, seq)
    assert seq % tm_p == 0 and seq % tm_f == 0

    assert bp % MLSTM_SEQS_PER_STEP == 0 and bs % SUBLANES == 0
    zeros_c = jnp.zeros((bp, M_HEADS, M_HEAD_DIM, M_HEAD_DIM), F32)
    zeros_n = jnp.zeros((bp, M_HEADS, M_HEAD_DIM), F32)
    zeros_m = jnp.zeros((bp, 1, M_HEADS), F32)

    outs = {k: [] for k in ("ks", "vs", "cvp", "cvs", "Cp", "Cs", "np", "ns", "mp", "ms")}
    kt_all = jnp.zeros((depth, bp, SB_WIDTH, seq), F32)
    vt_all = jnp.zeros((depth, bp, SB_WIDTH, seq), F32)
    mod_f = _ada(c_all, w_ada_f[None], b_ada_f[None], 0)
    gf = norm_f_g.reshape(1, d)
    for l in range(depth):
        mod = _ada(c_all, w_ada, b_ada, l)
        wt = _pack_w_in(w_in[l])
        wo, wg, wu, wd = (w_out[l].astype(BF16), w_gate[l].astype(BF16), w_up[l].astype(BF16),
                          w_down[l].astype(BF16))
        g1 = norm1_g[l].reshape(1, d)
        g2 = norm2_g[l].reshape(1, d)
        bg_row = jnp.pad(b_gate[l], (0, LANES - n_gate)).reshape(1, LANES)
        cb = conv_b[l].reshape(1, 2 * M_WIDTH)
        gh = head_norm_g[l].reshape(1, M_WIDTH)
        final = (mod_f, gf) if l == depth - 1 else None

        qc, kc, vm, om, gt, q, kb, vb, kt_all, vt_all, cs = _inproj_prompt(
            xp, mod, g1, wt, conv_w[l], cb, kt_all, vt_all, l, tm_p, seq, bs)
        r3 = lambda a: a.reshape(bp, seq, a.shape[-1])
        hm, c_new, n_new, m_new = _mlstm(r3(qc), r3(kc), r3(vm), r3(om), r3(gt), zeros_c, zeros_n,
                                         zeros_m, bg_row, gh)
        hs = _sb_prompt(r3(q), r3(kb), r3(vb), sb_bias[l])
        xp = _ffn(xp, hm.reshape(bp * seq, M_WIDTH), hs.reshape(bp * seq, SB_WIDTH), mod, g2,
                  wo, wg, wu, wd, "prompt", tm_f, seq, bs, final)
        outs["cvp"].append(cs[:, SUBLANES - (CONV_W - 1):, :])
        outs["Cp"].append(c_new)
        outs["np"].append(n_new)
        outs["mp"].append(m_new.reshape(bp, M_HEADS))

        qk, qc, kc, vm, om, gt, q, kf, vf = _inproj_sample(xs, mod, g1, wt, conv_w[l], cb, state_conv[l])
        hm, c_new, n_new, m_new = _mlstm_step(qc, kc, vm, om, gt, state_mlstm_C[l], state_mlstm_n[l],
                                              state_mlstm_m[l], bg_row, gh)
        hs = _sb_sample(l, q, cache_sb_k, cache_sb_v, page_table, sb_bias[l])
        xs = _ffn(xs, hm, hs, mod, g2, wo, wg, wu, wd, "sample", bs, 1, bs, final)
        outs["ks"].append(kf.reshape(bs, 1, SB_HEADS, SB_HEAD_DIM))
        outs["vs"].append(vf.reshape(bs, 1, SB_HEADS, SB_HEAD_DIM))
        outs["cvs"].append(jnp.concatenate([state_conv[l][:, 1:, :], qk[:, None, :]], axis=1))
        outs["Cs"].append(c_new)
        outs["ns"].append(n_new)
        outs["ms"].append(m_new.reshape(bs, M_HEADS))

    y_prompt = xp.reshape(bp, seq, d)
    y_sample = xs.reshape(bs, 1, d)
    st = lambda k: jnp.stack(outs[k])
    tok_major = lambda a: jnp.transpose(a.reshape(depth, bp, SB_HEADS, SB_HEAD_DIM, seq), (0, 1, 4, 2, 3))
    return (y_prompt, y_sample, tok_major(kt_all), tok_major(vt_all), st("ks"), st("vs"),
            st("cvp"), st("cvs"), st("Cp"), st("Cs"), st("np"), st("ns"), st("mp"), st("ms"))
```
